```python
import jax, jax.numpy as jnp
from jax import lax
import numpy as np

D_MODEL = 1024
BATCH = 4
SEQ = 8192
DEPTH = 1

RWKV_HEAD_DIM = 64
RWKV_WIDTH = D_MODEL // 2
RWKV_HEADS = RWKV_WIDTH // RWKV_HEAD_DIM
DECAY_LORA = 64
ICLR_LORA = 64
GATE_LORA = 160
SB_HEAD_DIM = 64
SB_WIDTH = D_MODEL // 2
SB_HEADS = SB_WIDTH // SB_HEAD_DIM
Q_BLOCK = 128
D_FF = 2688
CONV_WIDTH = 3
N_BRANCHES = 2
RMS_EPS = 1e-6
GN_EPS = 64e-5
L2_EPS = 1e-12

RWKV_IN = 3 * RWKV_WIDTH + DECAY_LORA + ICLR_LORA + GATE_LORA
SB_IN = 3 * SB_WIDTH
GATE_IN = N_BRANCHES * D_MODEL
IN_WIDTH = RWKV_IN + SB_IN + GATE_IN
N_MOD = 6

kernel_name = "hybrid_rwkv7_stickbreak_convffn_block"


def split_last(t, sizes):
    idx = [int(i) for i in np.cumsum(sizes)[:-1]]
    return jnp.split(t, idx, axis=-1)


def rms_norm(t, g):
    tf = t.astype(jnp.float32)
    y = tf * lax.rsqrt(jnp.mean(tf * tf, axis=-1, keepdims=True) + RMS_EPS)
    return (y * g.astype(jnp.float32)).astype(t.dtype)


def token_shift(f):
    return jnp.pad(f, ((0, 0), (1, 0), (0, 0)))[:, :-1]


def rwkv7_step(state, inp):
    r_t, w_t, k_t, v_t, kk_t, b_t = inp
    sa = jnp.einsum('bhvk,bhk->bhv', state, -kk_t)
    state = (state * w_t[:, :, None, :]
             + sa[..., None] * b_t[:, :, None, :]
             + v_t[..., None] * k_t[:, :, None, :])
    y_t = jnp.einsum('bhvk,bhk->bhv', state, r_t)
    return state, y_t


def rwkv7_time_mix(f, mu, w0, w2, a0, a2, g2, k_k, k_a, r_k, lnx_g, lnx_b):
    B, S, _ = f.shape
    H, N = RWKV_HEADS, RWKV_HEAD_DIM
    f32 = jnp.float32
    f = f + (token_shift(f) - f) * mu
    r, k, v, wd, ad, gd = split_last(
        f, (RWKV_WIDTH, RWKV_WIDTH, RWKV_WIDTH, DECAY_LORA, ICLR_LORA, GATE_LORA))
    w_log = -jax.nn.softplus(-(w0 + jnp.tanh(wd) @ w2)) - 0.5
    decay = jnp.exp(-jnp.exp(w_log.astype(f32)))
    a = jax.nn.sigmoid(a0 + ad @ a2)
    g = jax.nn.sigmoid(gd) @ g2
    kk = (k * k_k).reshape(B, S, H, N).astype(f32)
    kk = kk / jnp.maximum(jnp.sqrt(jnp.sum(kk * kk, axis=-1, keepdims=True)), L2_EPS)
    k = k * (1 + (a - 1) * k_a)

    def heads(t):
        return t.reshape(B, S, H, N).astype(f32)

    r_h, k_h, v_h, a_h, w_h = heads(r), heads(k), heads(v), heads(a), heads(decay)
    b_h = kk * a_h
    xs = tuple(jnp.moveaxis(t, 1, 0) for t in (r_h, w_h, k_h, v_h, kk, b_h))
    state0 = jnp.zeros((B, H, N, N), f32)
    _, y = lax.scan(rwkv7_step, state0, xs)
    y = jnp.moveaxis(y, 0, 1)
    mean = jnp.mean(y, axis=-1, keepdims=True)
    var = jnp.mean(jnp.square(y - mean), axis=-1, keepdims=True)
    y = ((y - mean) * lax.rsqrt(var + GN_EPS)).reshape(B, S, RWKV_WIDTH)
    y = y * lnx_g.astype(f32) + lnx_b.astype(f32)
    bonus = jnp.sum(r_h * k_h * r_k.astype(f32), axis=-1, keepdims=True) * v_h
    y = (y + bonus.reshape(B, S, RWKV_WIDTH)) * g.astype(f32)
    return y.astype(f.dtype)


def stick_breaking_attention(q, k, v):
    B, S, H, Dh = q.shape
    nb = S // Q_BLOCK
    scale = Dh ** -0.5
    qb = q.reshape(B, nb, Q_BLOCK, H, Dh).transpose(1, 0, 3, 2, 4)
    kf = k.transpose(0, 2, 1, 3)
    vf = v.transpose(0, 2, 1, 3)
    key_pos = jnp.arange(S)

    def block(args):
        q_blk, blk = args
        q_pos = blk * Q_BLOCK + jnp.arange(Q_BLOCK)
        z = jnp.einsum('bhqd,bhkd->bhqk', q_blk, kf).astype(jnp.float32) * scale
        causal = key_pos[None, :] < q_pos[:, None]
        log_beta = jax.nn.log_sigmoid(z)
        log_1mb = jnp.where(causal, jax.nn.log_sigmoid(-z), 0.0)
        survive = lax.cumsum(log_1mb, axis=3, reverse=True) - log_1mb
        A = jnp.where(causal, jnp.exp(log_beta + survive), 0.0)
        return jnp.einsum('bhqk,bhkd->bhqd', A.astype(vf.dtype), vf)

    out = lax.map(block, (qb, jnp.arange(nb)))
    return out.transpose(1, 0, 3, 2, 4).reshape(B, S, H * Dh)


def causal_depthwise_conv(u, w, b):
    C = u.shape[-1]
    y = lax.conv_general_dilated(
        u, w[:, None, :].astype(u.dtype), window_strides=(1,),
        padding=((CONV_WIDTH - 1, 0),),
        dimension_numbers=('NWC', 'WIO', 'NWC'),
        feature_group_count=C)
    return y + b


def setup_inputs(seed: int = 0) -> dict:
    key = jax.random.key(seed)
    ks = jax.random.split(key, 32)
    L, D = DEPTH, D_MODEL

    def nrm(k, shape, s):
        return jax.random.normal(k, shape, jnp.float32) * s

    return {
        "x": nrm(ks[0], (BATCH, SEQ, D), 1.0),
        "c": nrm(ks[1], (BATCH, D), 1.0),
        "w_ada": nrm(ks[2], (L, D, N_MOD * D), 0.5 * D ** -0.5),
        "b_ada": nrm(ks[3], (L, N_MOD * D), 0.01),
        "norm1_g": 1.0 + nrm(ks[4], (L, D), 0.02),
        "w_in": nrm(ks[5], (L, D, IN_WIDTH), D ** -0.5),
        "b_gate": nrm(ks[6], (L, GATE_IN), 0.01),
        "rwkv_mu": jax.random.uniform(ks[7], (L, RWKV_IN), jnp.float32),
        "rwkv_w0": jax.random.uniform(ks[8], (L, RWKV_WIDTH), jnp.float32, -6.0, 0.0),
        "rwkv_w2": nrm(ks[9], (L, DECAY_LORA, RWKV_WIDTH), 0.1 * DECAY_LORA ** -0.5),
        "rwkv_a0": nrm(ks[10], (L, RWKV_WIDTH), 0.1),
        "rwkv_a2": nrm(ks[11], (L, ICLR_LORA, RWKV_WIDTH), 0.1 * ICLR_LORA ** -0.5),
        "rwkv_g2": nrm(ks[12], (L, GATE_LORA, RWKV_WIDTH), GATE_LORA ** -0.5),
        "rwkv_k_k": 0.85 + nrm(ks[13], (L, RWKV_WIDTH), 0.02),
        "rwkv_k_a": 1.0 + nrm(ks[14], (L, RWKV_WIDTH), 0.02),
        "rwkv_r_k": -0.04 + nrm(ks[15], (L, RWKV_HEADS, RWKV_HEAD_DIM), 0.02),
        "rwkv_lnx_g": 1.0 + nrm(ks[16], (L, RWKV_WIDTH), 0.02),
        "rwkv_lnx_b": nrm(ks[17], (L, RWKV_WIDTH), 0.01),
        "sb_q_g": 1.0 + nrm(ks[18], (L, SB_HEAD_DIM), 0.02),
        "sb_k_g": 1.0 + nrm(ks[19], (L, SB_HEAD_DIM), 0.02),
        "w_o_rwkv": nrm(ks[20], (L, RWKV_WIDTH, D), RWKV_WIDTH ** -0.5),
        "w_o_sb": nrm(ks[21], (L, SB_WIDTH, D), SB_WIDTH ** -0.5),
        "w_out": nrm(ks[22], (L, D, D), D ** -0.5),
        "norm2_g": 1.0 + nrm(ks[23], (L, D), 0.02),
        "w_up": nrm(ks[24], (L, D, 2 * D_FF), D ** -0.5),
        "conv_w": nrm(ks[25], (L, CONV_WIDTH, 2 * D_FF), CONV_WIDTH ** -0.5),
        "conv_b": nrm(ks[26], (L, 2 * D_FF), 0.01),
        "w_down": nrm(ks[27], (L, D_FF, D), D_FF ** -0.5),
    }


def reference(x, c, w_ada, b_ada, norm1_g, w_in, b_gate, rwkv_mu, rwkv_w0, rwkv_w2,
              rwkv_a0, rwkv_a2, rwkv_g2, rwkv_k_k, rwkv_k_a, rwkv_r_k, rwkv_lnx_g,
              rwkv_lnx_b, sb_q_g, sb_k_g, w_o_rwkv, w_o_sb, w_out, norm2_g, w_up,
              conv_w, conv_b, w_down):
    B, S, D = x.shape
    for l in range(DEPTH):
        mod = jax.nn.silu(c) @ w_ada[l] + b_ada[l]
        shift1, scale1, gate1, shift2, scale2, gate2 = jnp.split(mod[:, None, :], N_MOD, axis=-1)

        h = rms_norm(x, norm1_g[l]) * (1 + scale1) + shift1
        proj = h @ w_in[l]
        f_rwkv, f_sb, f_gate = split_last(proj, (RWKV_IN, SB_IN, GATE_IN))

        y_rwkv = rwkv7_time_mix(f_rwkv, rwkv_mu[l], rwkv_w0[l], rwkv_w2[l], rwkv_a0[l],
                                rwkv_a2[l], rwkv_g2[l], rwkv_k_k[l], rwkv_k_a[l],
                                rwkv_r_k[l], rwkv_lnx_g[l], rwkv_lnx_b[l])

        q, k, v = split_last(f_sb, (SB_WIDTH, SB_WIDTH, SB_WIDTH))
        q = rms_norm(q.reshape(B, S, SB_HEADS, SB_HEAD_DIM), sb_q_g[l])
        k = rms_norm(k.reshape(B, S, SB_HEADS, SB_HEAD_DIM), sb_k_g[l])
        v = v.reshape(B, S, SB_HEADS, SB_HEAD_DIM)
        y_sb = stick_breaking_attention(q, k, v)

        gate_rwkv, gate_sb = jnp.split(jax.nn.sigmoid(f_gate + b_gate[l]), N_BRANCHES, axis=-1)
        merged = gate_rwkv * (y_rwkv @ w_o_rwkv[l]) + gate_sb * (y_sb @ w_o_sb[l])
        x = x + gate1 * (merged @ w_out[l])

        h2 = rms_norm(x, norm2_g[l]) * (1 + scale2) + shift2
        up = causal_depthwise_conv(h2 @ w_up[l], conv_w[l], conv_b[l])
        val, gt = jnp.split(up, 2, axis=-1)
        x = x + gate2 * ((jax.nn.silu(gt) * val) @ w_down[l])
    return x
```

```python
import functools

import jax
import jax.numpy as jnp
from jax import lax
from jax.experimental import pallas as pl
from jax.experimental.pallas import tpu as pltpu

F32 = jnp.float32
BF16 = jnp.bfloat16

HEAD_DIM = 64
N_HEADS = 8
MIX_WIDTH = HEAD_DIM * N_HEADS
DECAY_LORA = 64
ICLR_LORA = 64
GATE_LORA = 160
RWKV_PAD = 2048
CONV_WIDTH = 3
CONV_HALO = 8
RMS_EPS = 1e-6
GN_EPS = 64e-5
L2_EPS = 1e-12
CHUNK = 64

VMEM_LIMIT = 56 * 1024 * 1024


def _dot(a, b):
    return jnp.dot(a, b, preferred_element_type=F32)


def _dot_nt(a, b):
    return lax.dot_general(a, b, (((1,), (1,)), ((), ())), preferred_element_type=F32)


def _dot_tn(a, b):
    return lax.dot_general(a, b, (((0,), (0,)), ((), ())), preferred_element_type=F32)


def _sigmoid(x):
    return 1.0 / (1.0 + jnp.exp(-x))


def _softplus(x):
    return jnp.maximum(x, 0.0) + jnp.log(1.0 + jnp.exp(-jnp.abs(x)))


def _params(sem):
    return pltpu.CompilerParams(dimension_semantics=sem, vmem_limit_bytes=VMEM_LIMIT)


def _mod_kernel(c_ref, w_ref, b_ref, o_ref):
    c = c_ref[...]
    sc = (c * _sigmoid(c)).astype(BF16)
    o_ref[...] = _dot(sc, w_ref[...].astype(BF16)) + b_ref[...]


def _modulation(c, w_ada, b_ada, tn=1536):
    bsz, d = c.shape
    n = w_ada.shape[1]
    return pl.pallas_call(
        _mod_kernel,
        grid=(n // tn,),
        in_specs=[
            pl.BlockSpec((bsz, d), lambda j: (0, 0)),
            pl.BlockSpec((d, tn), lambda j: (0, j)),
            pl.BlockSpec((1, tn), lambda j: (0, j)),
        ],
        out_specs=pl.BlockSpec((bsz, tn), lambda j: (0, j)),
        out_shape=jax.ShapeDtypeStruct((bsz, n), F32),
        compiler_params=_params(("arbitrary",)),
        name="adaln_mod",
    )(c, w_ada, b_ada.reshape(1, n))


def _inproj_kernel(x_ref, mod_ref, g_ref, w_ref, bd_ref, qg_ref, kg_ref,
                   rw_ref, gate_ref, q_ref, k_ref, v_ref):
    x = x_ref[0]
    ms = jnp.mean(x * x, axis=-1, keepdims=True)
    h = x * lax.rsqrt(ms + RMS_EPS) * g_ref[...]
    h = h * (1.0 + mod_ref[0, 1:2, :]) + mod_ref[0, 0:1, :]
    hb = h.astype(BF16)
    cw = MIX_WIDTH
    for j in range(RWKV_PAD // cw):
        rw_ref[0, :, j * cw:(j + 1) * cw] = _dot(hb, w_ref[:, j * cw:(j + 1) * cw])
    off = RWKV_PAD
    for j in range(2 * x.shape[1] // cw):
        gate_ref[0, :, j * cw:(j + 1) * cw] = _dot(hb, w_ref[:, off + j * cw:off + (j + 1) * cw])
    off += 2 * x.shape[1]

    def head_norm(t, gain):
        msq = _dot((t * t).astype(BF16), bd_ref[...]) * (1.0 / HEAD_DIM)
        return t * lax.rsqrt(msq + RMS_EPS) * gain

    q = head_norm(_dot(hb, w_ref[:, off:off + cw]), qg_ref[...]) * (HEAD_DIM ** -0.5)
    k = head_norm(_dot(hb, w_ref[:, off + cw:off + 2 * cw]), kg_ref[...])
    v = _dot(hb, w_ref[:, off + 2 * cw:off + 3 * cw])
    for hd in range(N_HEADS):
        sl = slice(hd * HEAD_DIM, (hd + 1) * HEAD_DIM)
        q_ref[0, hd] = q[:, sl].astype(BF16)
        k_ref[0, hd] = k[:, sl].astype(BF16)
        v_ref[0, hd] = v[:, sl].astype(BF16)


def _in_projection(x, mod, norm1_g, w_all, bd_ones, qg, kg, tm=256):
    bsz, s, d = x.shape
    n_all = w_all.shape[1]
    head_shape = jax.ShapeDtypeStruct((bsz, N_HEADS, s, HEAD_DIM), BF16)
    head_spec = pl.BlockSpec((1, N_HEADS, tm, HEAD_DIM), lambda b, i: (b, 0, i, 0))
    return pl.pallas_call(
        _inproj_kernel,
        grid=(bsz, s // tm),
        in_specs=[
            pl.BlockSpec((1, tm, d), lambda b, i: (b, i, 0)),
            pl.BlockSpec((1, 6, d), lambda b, i: (b, 0, 0)),
            pl.BlockSpec((1, d), lambda b, i: (0, 0)),
            pl.BlockSpec((d, n_all), lambda b, i: (0, 0)),
            pl.BlockSpec((MIX_WIDTH, MIX_WIDTH), lambda b, i: (0, 0)),
            pl.BlockSpec((1, MIX_WIDTH), lambda b, i: (0, 0)),
            pl.BlockSpec((1, MIX_WIDTH), lambda b, i: (0, 0)),
        ],
        out_specs=[
            pl.BlockSpec((1, tm, RWKV_PAD), lambda b, i: (b, i, 0)),
            pl.BlockSpec((1, tm, 2 * d), lambda b, i: (b, i, 0)),
            head_spec, head_spec, head_spec,
        ],
        out_shape=[
            jax.ShapeDtypeStruct((bsz, s, RWKV_PAD), F32),
            jax.ShapeDtypeStruct((bsz, s, 2 * d), F32),
            head_shape, head_shape, head_shape,
        ],
        compiler_params=_params(("arbitrary", "arbitrary")),
        name="in_proj",
    )(x, mod, norm1_g, w_all, bd_ones, qg, kg)


def _rwkv_kernel(f_ref, mu_ref, w0_ref, a0_ref, kkw_ref, kaw_ref, rk_ref, lng_ref, lnb_ref,
                 wl_ref, wg_ref, bd_ref, cum_ref, tot_ref,
                 y_ref,
                 prev_s, h_s, at_s, rt_s, bt_s, kt_s, bh_s, kh_s, v_s, gc_s,
                 q1_s, q2_s, m_s, n_s, yn_s):
    ts = f_ref.shape[1]
    n_chunks = ts // CHUNK
    w = MIX_WIDTH

    @pl.when(pl.program_id(1) == 0)
    def _():
        prev_s[...] = jnp.zeros_like(prev_s)
        h_s[...] = jnp.zeros_like(h_s)

    f = f_ref[0]
    row = lax.broadcasted_iota(jnp.int32, f.shape, 0)
    shifted = jnp.where(row == 0, prev_s[...], pltpu.roll(f, 1, axis=0))
    prev_s[...] = f[ts - 1:ts, :]
    f = f + (shifted - f) * mu_ref[...]

    r = f[:, 0:w]
    k = f[:, w:2 * w]
    v = f[:, 2 * w:3 * w]
    lora_in = f[:, 3 * w:3 * w + 128]
    gate_in = f[:, 3 * w + 128:3 * w + 384]
    lane = lax.broadcasted_iota(jnp.int32, lora_in.shape, 1)
    lora_act = jnp.where(lane < DECAY_LORA, jnp.tanh(lora_in), lora_in)
    lora = _dot(lora_act.astype(BF16), wl_ref[...])
    g = _dot(_sigmoid(gate_in).astype(BF16), wg_ref[...])

    w_log = -_softplus(-(w0_ref[...] + lora[:, 0:w])) - 0.5
    ld = -jnp.exp(w_log)
    a = _sigmoid(a0_ref[...] + lora[:, w:2 * w])
    kk = k * kkw_ref[...]
    ss = _dot((kk * kk).astype(BF16), bd_ref[...])
    kkn = kk * lax.rsqrt(jnp.maximum(ss, L2_EPS * L2_EPS))
    k2 = k * (1.0 + (a - 1.0) * kaw_ref[...])
    b = kkn * a
    bonus = _dot((r * k2 * rk_ref[...]).astype(BF16), bd_ref[...]) * v

    ld_hi = ld.astype(BF16)
    ld_lo = (ld - ld_hi.astype(F32)).astype(BF16)
    cl = _dot(cum_ref[...], ld_hi) + _dot(cum_ref[...], ld_lo)
    cle = _dot(tot_ref[...], ld_hi) + _dot(tot_ref[...], ld_lo)
    e_neg = jnp.exp(-cl)
    e_end = jnp.exp(cle - cl)
    at_s[...] = (-kkn * jnp.exp(cl - ld)).astype(BF16)
    rt_s[...] = (r * jnp.exp(cl)).astype(BF16)
    bt_s[...] = (b * e_neg).astype(BF16)
    kt_s[...] = (k2 * e_neg).astype(BF16)
    bh_s[...] = (b * e_end).astype(BF16)
    kh_s[...] = (k2 * e_end).astype(BF16)
    v_s[...] = v.astype(BF16)
    gc_s[...] = jnp.exp(cle)

    ri = lax.broadcasted_iota(jnp.int32, (CHUNK, CHUNK), 0)
    ci = lax.broadcasted_iota(jnp.int32, (CHUNK, CHUNK), 1)
    strict = ri > ci
    incl = ri >= ci
    eye = ri == ci

    def chunk_prepare(c, _):
        rows = pl.ds(pl.multiple_of(c * CHUNK, CHUNK), CHUNK)
        for hd in range(N_HEADS):
            cols = slice(hd * HEAD_DIM, (hd + 1) * HEAD_DIM)
            at = at_s[rows, cols]
            rt = rt_s[rows, cols]
            bt = bt_s[rows, cols]
            kt = kt_s[rows, cols]
            bh = bh_s[rows, cols]
            kh = kh_s[rows, cols]
            vv = v_s[rows, cols]
            a_ab = jnp.where(strict, _dot_nt(at, bt), 0.0)
            a_ak = jnp.where(strict, _dot_nt(at, kt), 0.0)
            a_rb = jnp.where(incl, _dot_nt(rt, bt), 0.0).astype(BF16)
            a_rk = jnp.where(incl, _dot_nt(rt, kt), 0.0).astype(BF16)
            t = jnp.where(eye, 1.0, a_ab)
            p = a_ab
            for _ in range(5):
                pb = p.astype(BF16)
                p = _dot(pb, pb)
                t = t + _dot(t.astype(BF16), p.astype(BF16))
            tb = t.astype(BF16)
            p1 = _dot(tb, at)
            p2 = _dot(tb, _dot(a_ak.astype(BF16), vv).astype(BF16))
            p1b = p1.astype(BF16)
            p2b = p2.astype(BF16)
            q1 = rt.astype(F32) + _dot(a_rb, p1b)
            q2 = _dot(a_rb, p2b) + _dot(a_rk, vv)
            gc = gc_s[pl.ds(pl.multiple_of(c * CHUNK, CHUNK), 1), cols]
            m = jnp.where(eye, gc, 0.0) + _dot_tn(bh, p1b)
            n = _dot_tn(bh, p2b) + _dot_tn(kh, vv)
            q1_s[rows, cols] = q1.astype(BF16)
            q2_s[rows, cols] = q2
            m_s[rows, cols] = m.astype(BF16)
            n_s[rows, cols] = n
        return 0

    lax.fori_loop(0, n_chunks, chunk_prepare, 0)

    def chunk_scan(c, _):
        rows = pl.ds(pl.multiple_of(c * CHUNK, CHUNK), CHUNK)
        for hd in range(N_HEADS):
            cols = slice(hd * HEAD_DIM, (hd + 1) * HEAD_DIM)
            hb = h_s[hd].astype(BF16)
            y = _dot(q1_s[rows, cols], hb) + q2_s[rows, cols]
            h_s[hd] = _dot(m_s[rows, cols], hb) + n_s[rows, cols]
            mean = jnp.mean(y, axis=-1, keepdims=True)
            yc = y - mean
            var = jnp.mean(yc * yc, axis=-1, keepdims=True)
            yn_s[rows, cols] = yc * lax.rsqrt(var + GN_EPS)
        return 0

    lax.fori_loop(0, n_chunks, chunk_scan, 0)

    y = (yn_s[...] * lng_ref[...] + lnb_ref[...] + bonus) * g
    y_ref[0] = y.astype(BF16)


def _rwkv_mix(f_rwkv, mu, w0, a0, k_k, k_a, r_k, lnx_g, lnx_b, wl, wg, bd_ones, ts=256):
    bsz, s, fw = f_rwkv.shape
    w = MIX_WIDTH
    ri = jnp.arange(ts)[:, None]
    ci = jnp.arange(ts)[None, :]
    same = (ri // CHUNK) == (ci // CHUNK)
    cum = (same & (ci <= ri)).astype(BF16)
    tot = same.astype(BF16)
    vec = lambda n: pl.BlockSpec((1, n), lambda b, i: (0, 0))
    full = lambda a: pl.BlockSpec(a.shape, lambda b, i: (0, 0))
    return pl.pallas_call(
        _rwkv_kernel,
        grid=(bsz, s // ts),
        in_specs=[
            pl.BlockSpec((1, ts, fw), lambda b, i: (b, i, 0)),
            vec(fw), vec(w), vec(w), vec(w), vec(w), vec(w), vec(w), vec(w),
            full(wl), full(wg), full(bd_ones), full(cum), full(tot),
        ],
        out_specs=pl.BlockSpec((1, ts, w), lambda b, i: (b, i, 0)),
        out_shape=jax.ShapeDtypeStruct((bsz, s, w), BF16),
        scratch_shapes=[
            pltpu.VMEM((1, fw), F32),
            pltpu.VMEM((N_HEADS, HEAD_DIM, HEAD_DIM), F32),
            pltpu.VMEM((ts, w), BF16), pltpu.VMEM((ts, w), BF16), pltpu.VMEM((ts, w), BF16),
            pltpu.VMEM((ts, w), BF16), pltpu.VMEM((ts, w), BF16), pltpu.VMEM((ts, w), BF16),
            pltpu.VMEM((ts, w), BF16),
            pltpu.VMEM((ts, w), F32),
            pltpu.VMEM((ts, w), BF16), pltpu.VMEM((ts, w), F32),
            pltpu.VMEM((ts, w), BF16), pltpu.VMEM((ts, w), F32),
            pltpu.VMEM((ts, w), F32),
        ],
        compiler_params=_params(("arbitrary", "arbitrary")),
        name="rwkv7_mix",
    )(f_rwkv, mu, w0, a0, k_k, k_a, r_k, lnx_g, lnx_b, wl, wg, bd_ones, cum, tot)


def _sb_kernel(q_ref, k_ref, v_ref, suffix_ref, o_ref):
    t = q_ref.shape[2]
    qi = pl.program_id(2)
    q = q_ref[0, 0]
    suffix = suffix_ref[...]
    ri = lax.broadcasted_iota(jnp.int32, (t, t), 0)
    ci = lax.broadcasted_iota(jnp.int32, (t, t), 1)
    causal = ci < ri

    def block(j, carry, acc, masked):
        rows = pl.ds(pl.multiple_of(j * t, t), t)
        kj = k_ref[0, 0, rows, :]
        vj = v_ref[0, 0, rows, :]
        z = _dot_nt(q, kj)
        sp = _softplus(z)
        if masked:
            sp = jnp.where(causal, sp, 0.0)
        cs = _dot(sp.astype(BF16), suffix)
        a = jnp.exp(z - cs - carry)
        if masked:
            a = jnp.where(causal, a, 0.0)
        acc = acc + _dot(a.astype(BF16), vj)
        return carry + cs[:, 0:1], acc

    carry, acc = block(qi, jnp.zeros((t, 1), F32), jnp.zeros((t, HEAD_DIM), F32), True)

    def body(it, c):
        return block(qi - 1 - it, c[0], c[1], False)

    carry, acc = lax.fori_loop(0, qi, body, (carry, acc))
    o_ref[0, 0] = acc.astype(BF16)


def _sb_attention(q, k, v, t=256):
    bsz, nh, s, dh = q.shape
    suffix = (jnp.arange(t)[:, None] >= jnp.arange(t)[None, :]).astype(BF16)
    return pl.pallas_call(
        _sb_kernel,
        grid=(bsz, nh, s // t),
        in_specs=[
            pl.BlockSpec((1, 1, t, dh), lambda b, h, i: (b, h, i, 0)),
            pl.BlockSpec((1, 1, s, dh), lambda b, h, i: (b, h, 0, 0)),
            pl.BlockSpec((1, 1, s, dh), lambda b, h, i: (b, h, 0, 0)),
            pl.BlockSpec((t, t), lambda b, h, i: (0, 0)),
        ],
        out_specs=pl.BlockSpec((1, 1, t, dh), lambda b, h, i: (b, h, i, 0)),
        out_shape=jax.ShapeDtypeStruct((bsz, nh, s, dh), BF16),
        compiler_params=_params(("arbitrary", "arbitrary", "arbitrary")),
        name="stickbreak_attn",
    )(q, k, v, suffix)


def _merge_kernel(x_ref, yr_ref, ys_ref, gate_ref, bg_ref, mod_ref, wor_ref, wos_ref, wout_ref, o_ref):
    d = x_ref.shape[2]
    ys = jnp.concatenate([ys_ref[0, hd] for hd in range(N_HEADS)], axis=-1)
    gr = _sigmoid(gate_ref[0, :, 0:d] + bg_ref[:, 0:d])
    gs = _sigmoid(gate_ref[0, :, d:2 * d] + bg_ref[:, d:2 * d])
    merged = gr * _dot(yr_ref[0], wor_ref[...]) + gs * _dot(ys, wos_ref[...])
    o_ref[0] = x_ref[0] + mod_ref[0, 2:3, :] * _dot(merged.astype(BF16), wout_ref[...])


def _merge(x, y_rwkv, y_sb, gates, b_gate, mod, w_or, w_os, w_out, tm=512):
    bsz, s, d = x.shape
    full = lambda a: pl.BlockSpec(a.shape, lambda b, i: (0, 0))
    return pl.pallas_call(
        _merge_kernel,
        grid=(bsz, s // tm),
        in_specs=[
            pl.BlockSpec((1, tm, d), lambda b, i: (b, i, 0)),
            pl.BlockSpec((1, tm, MIX_WIDTH), lambda b, i: (b, i, 0)),
            pl.BlockSpec((1, N_HEADS, tm, HEAD_DIM), lambda b, i: (b, 0, i, 0)),
            pl.BlockSpec((1, tm, 2 * d), lambda b, i: (b, i, 0)),
            pl.BlockSpec((1, 2 * d), lambda b, i: (0, 0)),
            pl.BlockSpec((1, 6, d), lambda b, i: (b, 0, 0)),
            full(w_or), full(w_os), full(w_out),
        ],
        out_specs=pl.BlockSpec((1, tm, d), lambda b, i: (b, i, 0)),
        out_shape=jax.ShapeDtypeStruct((bsz, s, d), F32),
        compiler_params=_params(("arbitrary", "arbitrary")),
        name="merge_out",
    )(x, y_rwkv, y_sb, gates, b_gate, mod, w_or, w_os, w_out)


def _ffn_kernel(x_ref, halo_ref, mod_ref, g_ref, wup_ref, cw_ref, cb_ref, wdn_ref, o_ref, *, tf):
    tm = x_ref.shape[1]
    dff = wdn_ref.shape[0]
    x = x_ref[0]

    def norm_mod(t):
        ms = jnp.mean(t * t, axis=-1, keepdims=True)
        hh = t * lax.rsqrt(ms + RMS_EPS) * g_ref[...]
        return hh * (1.0 + mod_ref[0, 4:5, :]) + mod_ref[0, 3:4, :]

    h = norm_mod(x).astype(BF16)
    keep = (pl.program_id(1) > 0).astype(F32)
    hh = (norm_mod(halo_ref[0]) * keep).astype(BF16)
    hcat = jnp.concatenate([hh, h], axis=0)

    def conv(col0):
        cols = pl.ds(pl.multiple_of(col0, 128), tf)
        u = _dot(hcat, wup_ref[:, cols])
        cw = cw_ref[:, cols]
        out = cb_ref[:, cols] + cw[2:3, :] * u[CONV_HALO:, :]
        out = out + cw[1:2, :] * u[CONV_HALO - 1:CONV_HALO - 1 + tm, :]
        out = out + cw[0:1, :] * u[CONV_HALO - 2:CONV_HALO - 2 + tm, :]
        return out

    def body(j, acc):
        val = conv(j * tf)
        gt = conv(dff + j * tf)
        act = (gt * _sigmoid(gt) * val).astype(BF16)
        rows = pl.ds(pl.multiple_of(j * tf, 128), tf)
        return acc + _dot(act, wdn_ref[rows, :])

    acc = lax.fori_loop(0, dff // tf, body, jnp.zeros((tm, x.shape[1]), F32))
    o_ref[0] = x + mod_ref[0, 5:6, :] * acc


def _ffn(x, mod, norm2_g, w_up, conv_w, conv_b, w_down, tm=256, tf=384):
    bsz, s, d = x.shape
    dff = w_down.shape[0]
    hb = tm // CONV_HALO
    full = lambda a: pl.BlockSpec(a.shape, lambda b, i: (0, 0))
    return pl.pallas_call(
        functools.partial(_ffn_kernel, tf=tf),
        grid=(bsz, s // tm),
        in_specs=[
            pl.BlockSpec((1, tm, d), lambda b, i: (b, i, 0)),
            pl.BlockSpec((1, CONV_HALO, d), lambda b, i: (b, jnp.maximum(i * hb - 1, 0), 0)),
            pl.BlockSpec((1, 6, d), lambda b, i: (b, 0, 0)),
            pl.BlockSpec((1, d), lambda b, i: (0, 0)),
            full(w_up), full(conv_w), full(conv_b), full(w_down),
        ],
        out_specs=pl.BlockSpec((1, tm, d), lambda b, i: (b, i, 0)),
        out_shape=jax.ShapeDtypeStruct((bsz, s, d), F32),
        compiler_params=_params(("arbitrary", "arbitrary")),
        name="conv_ffn",
    )(x, x, mod, norm2_g, w_up, conv_w, conv_b, w_down)


def _layer(x, c, w_ada, b_ada, norm1_g, w_in, b_gate, rwkv_mu, rwkv_w0, rwkv_w2, rwkv_a0,
           rwkv_a2, rwkv_g2, rwkv_k_k, rwkv_k_a, rwkv_r_k, rwkv_lnx_g, rwkv_lnx_b, sb_q_g,
           sb_k_g, w_o_rwkv, w_o_sb, w_out, norm2_g, w_up, conv_w, conv_b, w_down):
    bsz, s, d = x.shape
    w = MIX_WIDTH
    rwkv_in = 3 * w + DECAY_LORA + ICLR_LORA + GATE_LORA
    sb_in = 3 * w
    pad = RWKV_PAD - rwkv_in

    w_all = jnp.concatenate([
        w_in[:, :rwkv_in], jnp.zeros((d, pad), F32),
        w_in[:, rwkv_in + sb_in:], w_in[:, rwkv_in:rwkv_in + sb_in]], axis=1).astype(BF16)
    mu = jnp.concatenate([rwkv_mu, jnp.zeros((pad,), F32)]).reshape(1, RWKV_PAD)
    wl = jnp.zeros((DECAY_LORA + ICLR_LORA, 2 * w), F32)
    wl = wl.at[:DECAY_LORA, :w].set(rwkv_w2).at[DECAY_LORA:, w:].set(rwkv_a2).astype(BF16)
    wg = jnp.zeros((256, w), F32).at[:GATE_LORA].set(rwkv_g2).astype(BF16)
    head_id = jnp.arange(w) // HEAD_DIM
    bd_ones = (head_id[:, None] == head_id[None, :]).astype(BF16)
    row = lambda a: a.reshape(1, -1)

    mod = _modulation(c, w_ada, b_ada).reshape(bsz, 6, d)
    f_rwkv, gates, q, k, v = _in_projection(
        x, mod, row(norm1_g), w_all, bd_ones,
        row(jnp.tile(sb_q_g, N_HEADS)), row(jnp.tile(sb_k_g, N_HEADS)))
    y_rwkv = _rwkv_mix(f_rwkv, mu, row(rwkv_w0), row(rwkv_a0), row(rwkv_k_k), row(rwkv_k_a),
                       row(rwkv_r_k), row(rwkv_lnx_g), row(rwkv_lnx_b), wl, wg, bd_ones)
    y_sb = _sb_attention(q, k, v)
    x = _merge(x, y_rwkv, y_sb, gates, row(b_gate), mod,
               w_o_rwkv.astype(BF16), w_o_sb.astype(BF16), w_out.astype(BF16))
    return _ffn(x, mod, row(norm2_g), w_up.astype(BF16), conv_w, row(conv_b), w_down.astype(BF16))


def kernel(x, c, w_ada, b_ada, norm1_g, w_in, b_gate, rwkv_mu, rwkv_w0, rwkv_w2, rwkv_a0, rwkv_a2, rwkv_g2, rwkv_k_k, rwkv_k_a, rwkv_r_k, rwkv_lnx_g, rwkv_lnx_b, sb_q_g, sb_k_g, w_o_rwkv, w_o_sb, w_out, norm2_g, w_up, conv_w, conv_b, w_down):
    params = (w_ada, b_ada, norm1_g, w_in, b_gate, rwkv_mu, rwkv_w0, rwkv_w2, rwkv_a0, rwkv_a2,
              rwkv_g2, rwkv_k_k, rwkv_k_a, rwkv_r_k, rwkv_lnx_g, rwkv_lnx_b, sb_q_g, sb_k_g,
              w_o_rwkv, w_o_sb, w_out, norm2_g, w_up, conv_w, conv_b, w_down)
    for layer in range(w_ada.shape[0]):
        x = _layer(x, c, *(p[layer] for p in params))
    return x
```

```python
import functools

import jax
import jax.numpy as jnp
from jax import lax
from jax.experimental import pallas as pl
from jax.experimental.pallas import tpu as pltpu

F32 = jnp.float32
BF16 = jnp.bfloat16

HEAD_DIM = 64
N_HEADS = 8
MIX_WIDTH = HEAD_DIM * N_HEADS
DECAY_LORA = 64
ICLR_LORA = 64
GATE_LORA = 160
RWKV_PAD = 2048
CONV_WIDTH = 3
CONV_HALO = 8
RMS_EPS = 1e-6
GN_EPS = 64e-5
L2_EPS = 1e-12
CHUNK = 64
LOG2E = 1.4426950408889634
MASKED_SCORE = -1e30
SB_UNROLL = 2

VMEM_LIMIT = 56 * 1024 * 1024


def _dot(a, b):
    return jnp.dot(a, b, preferred_element_type=F32)


def _dot_nt(a, b):
    return lax.dot_general(a, b, (((1,), (1,)), ((), ())), preferred_element_type=F32)


def _bmm(a, b):
    return lax.dot_general(a, b, (((2,), (1,)), ((0,), (0,))), preferred_element_type=F32)


def _bmm_nt(a, b):
    return lax.dot_general(a, b, (((2,), (2,)), ((0,), (0,))), preferred_element_type=F32)


def _bmm_tn(a, b):
    return lax.dot_general(a, b, (((1,), (1,)), ((0,), (0,))), preferred_element_type=F32)


def _sigmoid(x):
    return 1.0 / (1.0 + jnp.exp(-x))


def _softplus(x):
    return jnp.maximum(x, 0.0) + jnp.log(1.0 + jnp.exp(-jnp.abs(x)))


def _params(sem):
    return pltpu.CompilerParams(dimension_semantics=sem, vmem_limit_bytes=VMEM_LIMIT)


def _mod_kernel(c_ref, w_ref, b_ref, o_ref):
    c = c_ref[...]
    sc = (c * _sigmoid(c)).astype(BF16)
    o_ref[...] = _dot(sc, w_ref[...].astype(BF16)) + b_ref[...]


def _modulation(c, w_ada, b_ada, tn=1536):
    bsz, d = c.shape
    n = w_ada.shape[1]
    return pl.pallas_call(
        _mod_kernel,
        grid=(n // tn,),
        in_specs=[
            pl.BlockSpec((bsz, d), lambda j: (0, 0)),
            pl.BlockSpec((d, tn), lambda j: (0, j)),
            pl.BlockSpec((1, tn), lambda j: (0, j)),
        ],
        out_specs=pl.BlockSpec((bsz, tn), lambda j: (0, j)),
        out_shape=jax.ShapeDtypeStruct((bsz, n), F32),
        compiler_params=_params(("arbitrary",)),
        name="adaln_mod",
    )(c, w_ada, b_ada.reshape(1, n))


def _inproj_kernel(x_ref, mod_ref, g_ref, w_ref, bd_ref, qg_ref, kg_ref,
                   rw_ref, gate_ref, q_ref, k_ref, v_ref):
    x = x_ref[0]
    ms = jnp.mean(x * x, axis=-1, keepdims=True)
    h = x * lax.rsqrt(ms + RMS_EPS) * g_ref[...]
    h = h * (1.0 + mod_ref[0, 1:2, :]) + mod_ref[0, 0:1, :]
    hb = h.astype(BF16)
    cw = MIX_WIDTH
    for j in range(RWKV_PAD // cw):
        rw_ref[0, :, j * cw:(j + 1) * cw] = _dot(hb, w_ref[:, j * cw:(j + 1) * cw])
    off = RWKV_PAD
    for j in range(2 * x.shape[1] // cw):
        gate_ref[0, :, j * cw:(j + 1) * cw] = _dot(hb, w_ref[:, off + j * cw:off + (j + 1) * cw])
    off += 2 * x.shape[1]

    def head_norm(t, gain):
        msq = _dot((t * t).astype(BF16), bd_ref[...]) * (1.0 / HEAD_DIM)
        return t * lax.rsqrt(msq + RMS_EPS) * gain

    q = head_norm(_dot(hb, w_ref[:, off:off + cw]), qg_ref[...]) * (HEAD_DIM ** -0.5)
    k = head_norm(_dot(hb, w_ref[:, off + cw:off + 2 * cw]), kg_ref[...])
    q_ref[0] = q.astype(BF16)
    k_ref[0] = k.astype(BF16)
    v_ref[0] = _dot(hb, w_ref[:, off + 2 * cw:off + 3 * cw]).astype(BF16)


def _in_projection(x, mod, norm1_g, w_all, bd_ones, qg, kg, tm=256):
    bsz, s, d = x.shape
    n_all = w_all.shape[1]
    head_shape = jax.ShapeDtypeStruct((bsz, s, MIX_WIDTH), BF16)
    head_spec = pl.BlockSpec((1, tm, MIX_WIDTH), lambda b, i: (b, i, 0))
    return pl.pallas_call(
        _inproj_kernel,
        grid=(bsz, s // tm),
        in_specs=[
            pl.BlockSpec((1, tm, d), lambda b, i: (b, i, 0)),
            pl.BlockSpec((1, 6, d), lambda b, i: (b, 0, 0)),
            pl.BlockSpec((1, d), lambda b, i: (0, 0)),
            pl.BlockSpec((d, n_all), lambda b, i: (0, 0)),
            pl.BlockSpec((MIX_WIDTH, MIX_WIDTH), lambda b, i: (0, 0)),
            pl.BlockSpec((1, MIX_WIDTH), lambda b, i: (0, 0)),
            pl.BlockSpec((1, MIX_WIDTH), lambda b, i: (0, 0)),
        ],
        out_specs=[
            pl.BlockSpec((1, tm, RWKV_PAD), lambda b, i: (b, i, 0)),
            pl.BlockSpec((1, tm, 2 * d), lambda b, i: (b, i, 0)),
            head_spec, head_spec, head_spec,
        ],
        out_shape=[
            jax.ShapeDtypeStruct((bsz, s, RWKV_PAD), F32),
            jax.ShapeDtypeStruct((bsz, s, 2 * d), F32),
            head_shape, head_shape, head_shape,
        ],
        compiler_params=_params(("arbitrary", "arbitrary")),
        name="in_proj",
    )(x, mod, norm1_g, w_all, bd_ones, qg, kg)


def _rwkv_kernel(f_ref, mu_ref, w0_ref, a0_ref, kkw_ref, kaw_ref, rk_ref, lng_ref, lnb_ref,
                 wl_ref, wg_ref, bd_ref, cum_ref, tot_ref,
                 y_ref,
                 prev_s, h_s, at_s, rt_s, bt_s, kt_s, bh_s, kh_s, v_s, gc_s, yn_s):
    ts = f_ref.shape[1]
    n_chunks = ts // CHUNK
    w = MIX_WIDTH

    @pl.when(pl.program_id(1) == 0)
    def _():
        prev_s[...] = jnp.zeros_like(prev_s)
        h_s[...] = jnp.zeros_like(h_s)

    f = f_ref[0]
    row = lax.broadcasted_iota(jnp.int32, f.shape, 0)
    shifted = jnp.where(row == 0, prev_s[...], pltpu.roll(f, 1, axis=0))
    prev_s[...] = f[ts - 1:ts, :]
    f = f + (shifted - f) * mu_ref[...]

    r = f[:, 0:w]
    k = f[:, w:2 * w]
    v = f[:, 2 * w:3 * w]
    lora_in = f[:, 3 * w:3 * w + 128]
    gate_in = f[:, 3 * w + 128:3 * w + 384]
    lane = lax.broadcasted_iota(jnp.int32, lora_in.shape, 1)
    lora_act = jnp.where(lane < DECAY_LORA, jnp.tanh(lora_in), lora_in)
    lora = _dot(lora_act.astype(BF16), wl_ref[...])
    g = _dot(_sigmoid(gate_in).astype(BF16), wg_ref[...])

    w_log = -_softplus(-(w0_ref[...] + lora[:, 0:w])) - 0.5
    ld = -jnp.exp(w_log)
    a = _sigmoid(a0_ref[...] + lora[:, w:2 * w])
    kk = k * kkw_ref[...]
    ss = _dot((kk * kk).astype(BF16), bd_ref[...])
    kkn = kk * lax.rsqrt(jnp.maximum(ss, L2_EPS * L2_EPS))
    k2 = k * (1.0 + (a - 1.0) * kaw_ref[...])
    b = kkn * a
    bonus = _dot((r * k2 * rk_ref[...]).astype(BF16), bd_ref[...]) * v

    ld_hi = ld.astype(BF16)
    ld_lo = (ld - ld_hi.astype(F32)).astype(BF16)
    cl = _dot(cum_ref[...], ld_hi) + _dot(cum_ref[...], ld_lo)
    cle = _dot(tot_ref[...], ld_hi) + _dot(tot_ref[...], ld_lo)
    e_neg = jnp.exp(-cl)
    e_end = jnp.exp(cle - cl)
    at_s[...] = (-kkn * jnp.exp(cl - ld)).astype(BF16)
    rt_s[...] = (r * jnp.exp(cl)).astype(BF16)
    bt_s[...] = (b * e_neg).astype(BF16)
    kt_s[...] = (k2 * e_neg).astype(BF16)
    bh_s[...] = (b * e_end).astype(BF16)
    kh_s[...] = (k2 * e_end).astype(BF16)
    v_s[...] = v.astype(BF16)
    gc_s[...] = jnp.exp(cle)

    ri = lax.broadcasted_iota(jnp.int32, (CHUNK, CHUNK), 0)
    ci = lax.broadcasted_iota(jnp.int32, (CHUNK, CHUNK), 1)
    strict = ri > ci
    incl = ri >= ci
    eye = ri == ci

    def blocks(ref):
        return jnp.stack([ref[c * CHUNK:(c + 1) * CHUNK, hd * HEAD_DIM:(hd + 1) * HEAD_DIM]
                          for c in range(n_chunks) for hd in range(N_HEADS)])

    at, rt, bt, kt, bh, kh, vv = (blocks(s) for s in (at_s, rt_s, bt_s, kt_s, bh_s, kh_s, v_s))
    a_ab = jnp.where(strict, _bmm_nt(at, bt), 0.0)
    a_ak = jnp.where(strict, _bmm_nt(at, kt), 0.0).astype(BF16)
    a_rb = jnp.where(incl, _bmm_nt(rt, bt), 0.0).astype(BF16)
    a_rk = jnp.where(incl, _bmm_nt(rt, kt), 0.0).astype(BF16)
    t = jnp.where(eye, 1.0, a_ab)
    p = a_ab
    for _ in range(5):
        pb = p.astype(BF16)
        p = _bmm(pb, pb)
        t = t + _bmm(t.astype(BF16), p.astype(BF16))
    tb = t.astype(BF16)
    p1 = _bmm(tb, at).astype(BF16)
    p2 = _bmm(tb, _bmm(a_ak, vv).astype(BF16)).astype(BF16)
    q1 = (rt.astype(F32) + _bmm(a_rb, p1)).astype(BF16)
    q2 = _bmm(a_rb, p2) + _bmm(a_rk, vv)
    gc = jnp.stack([gc_s[c * CHUNK:c * CHUNK + 1, hd * HEAD_DIM:(hd + 1) * HEAD_DIM]
                    for c in range(n_chunks) for hd in range(N_HEADS)])
    m = (jnp.where(eye, gc, 0.0) + _bmm_tn(bh, p1)).astype(BF16)
    n = _bmm_tn(bh, p2) + _bmm_tn(kh, vv)

    h = h_s[...]
    for c in range(n_chunks):
        sl = slice(c * N_HEADS, (c + 1) * N_HEADS)
        hb = h.astype(BF16)
        y = _bmm(q1[sl], hb) + q2[sl]
        h = _bmm(m[sl], hb) + n[sl]
        mean = jnp.mean(y, axis=-1, keepdims=True)
        yc = y - mean
        var = jnp.mean(yc * yc, axis=-1, keepdims=True)
        yn = yc * lax.rsqrt(var + GN_EPS)
        for hd in range(N_HEADS):
            yn_s[c * CHUNK:(c + 1) * CHUNK, hd * HEAD_DIM:(hd + 1) * HEAD_DIM] = yn[hd]
    h_s[...] = h

    y = (yn_s[...] * lng_ref[...] + lnb_ref[...] + bonus) * g
    y_ref[0] = y.astype(BF16)


def _rwkv_mix(f_rwkv, mu, w0, a0, k_k, k_a, r_k, lnx_g, lnx_b, wl, wg, bd_ones, ts=256):
    bsz, s, fw = f_rwkv.shape
    w = MIX_WIDTH
    ri = jnp.arange(ts)[:, None]
    ci = jnp.arange(ts)[None, :]
    same = (ri // CHUNK) == (ci // CHUNK)
    cum = (same & (ci <= ri)).astype(BF16)
    tot = same.astype(BF16)
    vec = lambda n: pl.BlockSpec((1, n), lambda b, i: (0, 0))
    full = lambda a: pl.BlockSpec(a.shape, lambda b, i: (0, 0))
    return pl.pallas_call(
        _rwkv_kernel,
        grid=(bsz, s // ts),
        in_specs=[
            pl.BlockSpec((1, ts, fw), lambda b, i: (b, i, 0)),
            vec(fw), vec(w), vec(w), vec(w), vec(w), vec(w), vec(w), vec(w),
            full(wl), full(wg), full(bd_ones), full(cum), full(tot),
        ],
        out_specs=pl.BlockSpec((1, ts, w), lambda b, i: (b, i, 0)),
        out_shape=jax.ShapeDtypeStruct((bsz, s, w), BF16),
        scratch_shapes=[
            pltpu.VMEM((1, fw), F32),
            pltpu.VMEM((N_HEADS, HEAD_DIM, HEAD_DIM), F32),
            pltpu.VMEM((ts, w), BF16), pltpu.VMEM((ts, w), BF16), pltpu.VMEM((ts, w), BF16),
            pltpu.VMEM((ts, w), BF16), pltpu.VMEM((ts, w), BF16), pltpu.VMEM((ts, w), BF16),
            pltpu.VMEM((ts, w), BF16),
            pltpu.VMEM((ts, w), F32),
            pltpu.VMEM((ts, w), F32),
        ],
        compiler_params=_params(("arbitrary", "arbitrary")),
        name="rwkv7_mix",
    )(f_rwkv, mu, w0, a0, k_k, k_a, r_k, lnx_g, lnx_b, wl, wg, bd_ones, cum, tot)


def _sb_kernel(q_ref, k_ref, v_ref, suffix_ref, o_ref, z_s, sp_s, acc_s, carry_s):
    tq = q_ref.shape[1]
    tk = suffix_ref.shape[0]
    ratio = tq // tk
    qi = pl.program_id(2)
    suffix = suffix_ref[...]
    lane = lax.broadcasted_iota(jnp.int32, (tq, 2 * HEAD_DIM), 1)
    qp = q_ref[0]
    zero = jnp.zeros_like(qp)
    q_heads = (jnp.where(lane < HEAD_DIM, qp, zero), jnp.where(lane >= HEAD_DIM, qp, zero))

    def key_block(n):
        j = jnp.maximum(qi * ratio + (ratio - 1) - n, 0)
        return j, pl.ds(pl.multiple_of(j * tk, tk), tk)

    def first_row(n):
        return (ratio - 1 - n) * tk if is_own(n) else 0

    def is_own(n):
        return isinstance(n, int) and n < ratio

    def scores(n, slot):
        j, rows = key_block(n)
        r0 = first_row(n)
        kb = k_ref[0, rows, :]
        for hd in range(2):
            z = _dot_nt(q_heads[hd][r0:, :], kb)
            sp = jnp.maximum(z, 0.0) + jnp.log(1.0 + jnp.exp2(jnp.abs(z) * (-LOG2E)))
            if is_own(n):
                ri = lax.broadcasted_iota(jnp.int32, z.shape, 0)
                ci = lax.broadcasted_iota(jnp.int32, z.shape, 1)
                causal = ci < ri
                sp = jnp.where(causal, sp, 0.0)
                z = jnp.where(causal, z, MASKED_SCORE)
            z_s[hd, slot, r0:, :] = z
            sp_s[hd, slot, r0:, :] = sp.astype(BF16)

    def accumulate(n, slot):
        _, rows = key_block(n)
        r0 = first_row(n)
        vb = v_ref[0, rows, :]
        for hd in range(2):
            cs = _dot(sp_s[hd, slot, r0:, :], suffix)
            carry = carry_s[hd, r0:, :]
            a = jnp.exp(z_s[hd, slot, r0:, :] - cs - jnp.concatenate([carry] * (tk // 128), axis=1))
            acc_s[hd, r0:, :] += _dot(a.astype(BF16), vb)
            carry_s[hd, r0:, :] = carry + jnp.broadcast_to(cs[:, 0:1], carry.shape)

    acc_s[...] = jnp.zeros_like(acc_s)
    carry_s[...] = jnp.zeros_like(carry_s)
    scores(0, 0)
    for n in range(ratio):
        scores(n + 1, (n + 1) % 2)
        accumulate(n, n % 2)

    def group(p, _):
        n0 = ratio + SB_UNROLL * p
        for u in range(SB_UNROLL):
            scores(n0 + u + 1, (u + 1) % 2)
            accumulate(n0 + u, u % 2)
        return 0

    lax.fori_loop(0, qi * (ratio // SB_UNROLL), group, 0)
    o_ref[0] = jnp.where(lane < HEAD_DIM, acc_s[0], acc_s[1]).astype(BF16)


def _sb_attention(q, k, v, tq=1024, tk=256):
    bsz, s, w = q.shape
    pw = 2 * HEAD_DIM
    assert (tq // tk) % SB_UNROLL == 0 and SB_UNROLL % 2 == 0
    suffix = (jnp.arange(tk)[:, None] >= jnp.arange(tk)[None, :]).astype(BF16)
    return pl.pallas_call(
        _sb_kernel,
        grid=(bsz, w // pw, s // tq),
        in_specs=[
            pl.BlockSpec((1, tq, pw), lambda b, h, i: (b, i, h)),
            pl.BlockSpec((1, s, pw), lambda b, h, i: (b, 0, h)),
            pl.BlockSpec((1, s, pw), lambda b, h, i: (b, 0, h)),
            pl.BlockSpec((tk, tk), lambda b, h, i: (0, 0)),
        ],
        out_specs=pl.BlockSpec((1, tq, pw), lambda b, h, i: (b, i, h)),
        out_shape=jax.ShapeDtypeStruct((bsz, s, w), BF16),
        scratch_shapes=[
            pltpu.VMEM((2, 2, tq, tk), F32),
            pltpu.VMEM((2, 2, tq, tk), BF16),
            pltpu.VMEM((2, tq, pw), F32),
            pltpu.VMEM((2, tq, pw), F32),
        ],
        compiler_params=_params(("arbitrary", "arbitrary", "arbitrary")),
        name="stickbreak_attn",
    )(q, k, v, suffix)


def _merge_kernel(x_ref, yr_ref, ys_ref, gate_ref, bg_ref, mod_ref, wor_ref, wos_ref, wout_ref, o_ref):
    d = x_ref.shape[2]
    gr = _sigmoid(gate_ref[0, :, 0:d] + bg_ref[:, 0:d])
    gs = _sigmoid(gate_ref[0, :, d:2 * d] + bg_ref[:, d:2 * d])
    merged = gr * _dot(yr_ref[0], wor_ref[...]) + gs * _dot(ys_ref[0], wos_ref[...])
    o_ref[0] = x_ref[0] + mod_ref[0, 2:3, :] * _dot(merged.astype(BF16), wout_ref[...])


def _merge(x, y_rwkv, y_sb, gates, b_gate, mod, w_or, w_os, w_out, tm=512):
    bsz, s, d = x.shape
    full = lambda a: pl.BlockSpec(a.shape, lambda b, i: (0, 0))
    return pl.pallas_call(
        _merge_kernel,
        grid=(bsz, s // tm),
        in_specs=[
            pl.BlockSpec((1, tm, d), lambda b, i: (b, i, 0)),
            pl.BlockSpec((1, tm, MIX_WIDTH), lambda b, i: (b, i, 0)),
            pl.BlockSpec((1, tm, MIX_WIDTH), lambda b, i: (b, i, 0)),
            pl.BlockSpec((1, tm, 2 * d), lambda b, i: (b, i, 0)),
            pl.BlockSpec((1, 2 * d), lambda b, i: (0, 0)),
            pl.BlockSpec((1, 6, d), lambda b, i: (b, 0, 0)),
            full(w_or), full(w_os), full(w_out),
        ],
        out_specs=pl.BlockSpec((1, tm, d), lambda b, i: (b, i, 0)),
        out_shape=jax.ShapeDtypeStruct((bsz, s, d), F32),
        compiler_params=_params(("arbitrary", "arbitrary")),
        name="merge_out",
    )(x, y_rwkv, y_sb, gates, b_gate, mod, w_or, w_os, w_out)


def _ffn_kernel(x_ref, halo_ref, mod_ref, g_ref, wup_ref, cw_ref, cb_ref, wdn_ref, o_ref, *, tf):
    tm = x_ref.shape[1]
    dff = wdn_ref.shape[0]
    x = x_ref[0]

    def norm_mod(t):
        ms = jnp.mean(t * t, axis=-1, keepdims=True)
        hh = t * lax.rsqrt(ms + RMS_EPS) * g_ref[...]
        return hh * (1.0 + mod_ref[0, 4:5, :]) + mod_ref[0, 3:4, :]

    h = norm_mod(x).astype(BF16)
    keep = (pl.program_id(1) > 0).astype(F32)
    hh = (norm_mod(halo_ref[0]) * keep).astype(BF16)
    hcat = jnp.concatenate([hh, h], axis=0)

    def conv(col0):
        cols = pl.ds(pl.multiple_of(col0, 128), tf)
        u = _dot(hcat, wup_ref[:, cols])
        cw = cw_ref[:, cols]
        out = cb_ref[:, cols] + cw[2:3, :] * u[CONV_HALO:, :]
        out = out + cw[1:2, :] * u[CONV_HALO - 1:CONV_HALO - 1 + tm, :]
        out = out + cw[0:1, :] * u[CONV_HALO - 2:CONV_HALO - 2 + tm, :]
        return out

    def body(j, acc):
        val = conv(j * tf)
        gt = conv(dff + j * tf)
        act = (gt * _sigmoid(gt) * val).astype(BF16)
        rows = pl.ds(pl.multiple_of(j * tf, 128), tf)
        return acc + _dot(act, wdn_ref[rows, :])

    acc = lax.fori_loop(0, dff // tf, body, jnp.zeros((tm, x.shape[1]), F32))
    o_ref[0] = x + mod_ref[0, 5:6, :] * acc


def _ffn(x, mod, norm2_g, w_up, conv_w, conv_b, w_down, tm=256, tf=384):
    bsz, s, d = x.shape
    dff = w_down.shape[0]
    hb = tm // CONV_HALO
    full = lambda a: pl.BlockSpec(a.shape, lambda b, i: (0, 0))
    return pl.pallas_call(
        functools.partial(_ffn_kernel, tf=tf),
        grid=(bsz, s // tm),
        in_specs=[
            pl.BlockSpec((1, tm, d), lambda b, i: (b, i, 0)),
            pl.BlockSpec((1, CONV_HALO, d), lambda b, i: (b, jnp.maximum(i * hb - 1, 0), 0)),
            pl.BlockSpec((1, 6, d), lambda b, i: (b, 0, 0)),
            pl.BlockSpec((1, d), lambda b, i: (0, 0)),
            full(w_up), full(conv_w), full(conv_b), full(w_down),
        ],
        out_specs=pl.BlockSpec((1, tm, d), lambda b, i: (b, i, 0)),
        out_shape=jax.ShapeDtypeStruct((bsz, s, d), F32),
        compiler_params=_params(("arbitrary", "arbitrary")),
        name="conv_ffn",
    )(x, x, mod, norm2_g, w_up, conv_w, conv_b, w_down)


def _layer(x, c, w_ada, b_ada, norm1_g, w_in, b_gate, rwkv_mu, rwkv_w0, rwkv_w2, rwkv_a0,
           rwkv_a2, rwkv_g2, rwkv_k_k, rwkv_k_a, rwkv_r_k, rwkv_lnx_g, rwkv_lnx_b, sb_q_g,
           sb_k_g, w_o_rwkv, w_o_sb, w_out, norm2_g, w_up, conv_w, conv_b, w_down):
    bsz, s, d = x.shape
    w = MIX_WIDTH
    rwkv_in = 3 * w + DECAY_LORA + ICLR_LORA + GATE_LORA
    sb_in = 3 * w
    pad = RWKV_PAD - rwkv_in

    w_all = jnp.concatenate([
        w_in[:, :rwkv_in], jnp.zeros((d, pad), F32),
        w_in[:, rwkv_in + sb_in:], w_in[:, rwkv_in:rwkv_in + sb_in]], axis=1).astype(BF16)
    mu = jnp.concatenate([rwkv_mu, jnp.zeros((pad,), F32)]).reshape(1, RWKV_PAD)
    wl = jnp.zeros((DECAY_LORA + ICLR_LORA, 2 * w), F32)
    wl = wl.at[:DECAY_LORA, :w].set(rwkv_w2).at[DECAY_LORA:, w:].set(rwkv_a2).astype(BF16)
    wg = jnp.zeros((256, w), F32).at[:GATE_LORA].set(rwkv_g2).astype(BF16)
    head_id = jnp.arange(w) // HEAD_DIM
    bd_ones = (head_id[:, None] == head_id[None, :]).astype(BF16)
    row = lambda a: a.reshape(1, -1)

    mod = _modulation(c, w_ada, b_ada).reshape(bsz, 6, d)
    f_rwkv, gates, q, k, v = _in_projection(
        x, mod, row(norm1_g), w_all, bd_ones,
        row(jnp.tile(sb_q_g, N_HEADS)), row(jnp.tile(sb_k_g, N_HEADS)))
    y_rwkv = _rwkv_mix(f_rwkv, mu, row(rwkv_w0), row(rwkv_a0), row(rwkv_k_k), row(rwkv_k_a),
                       row(rwkv_r_k), row(rwkv_lnx_g), row(rwkv_lnx_b), wl, wg, bd_ones)
    y_sb = _sb_attention(q, k, v)
    x = _merge(x, y_rwkv, y_sb, gates, row(b_gate), mod,
               w_o_rwkv.astype(BF16), w_o_sb.astype(BF16), w_out.astype(BF16))
    return _ffn(x, mod, row(norm2_g), w_up.astype(BF16), conv_w, row(conv_b), w_down.astype(BF16))


def kernel(x, c, w_ada, b_ada, norm1_g, w_in, b_gate, rwkv_mu, rwkv_w0, rwkv_w2, rwkv_a0, rwkv_a2, rwkv_g2, rwkv_k_k, rwkv_k_a, rwkv_r_k, rwkv_lnx_g, rwkv_lnx_b, sb_q_g, sb_k_g, w_o_rwkv, w_o_sb, w_out, norm2_g, w_up, conv_w, conv_b, w_down):
    params = (w_ada, b_ada, norm1_g, w_in, b_gate, rwkv_mu, rwkv_w0, rwkv_w2, rwkv_a0, rwkv_a2,
              rwkv_g2, rwkv_k_k, rwkv_k_a, rwkv_r_k, rwkv_lnx_g, rwkv_lnx_b, sb_q_g, sb_k_g,
              w_o_rwkv, w_o_sb, w_out, norm2_g, w_up, conv_w, conv_b, w_down)
    for layer in range(w_ada.shape[0]):
        x = _layer(x, c, *(p[layer] for p in params))
    return x
```

```python
import functools

import jax
import jax.numpy as jnp
from jax import lax
from jax.experimental import pallas as pl
from jax.experimental.pallas import tpu as pltpu

F32 = jnp.float32
BF16 = jnp.bfloat16

HEAD_DIM = 64
N_HEADS = 8
MIX_WIDTH = HEAD_DIM * N_HEADS
DECAY_LORA = 64
ICLR_LORA = 64
GATE_LORA = 160
RWKV_PAD = 2048
CONV_WIDTH = 3
CONV_HALO = 8
RMS_EPS = 1e-6
GN_EPS = 64e-5
L2_EPS = 1e-12
CHUNK = 64
LOG2E = 1.4426950408889634
MASKED_SCORE = -1e30
SB_UNROLL = 4
MXU_TILE = 256

VMEM_LIMIT = 56 * 1024 * 1024


def _dot(a, b):
    return jnp.dot(a, b, preferred_element_type=F32)


def _dot_nt(a, b):
    return lax.dot_general(a, b, (((1,), (1,)), ((), ())), preferred_element_type=F32)


def _bmm(a, b):
    return lax.dot_general(a, b, (((2,), (1,)), ((0,), (0,))), preferred_element_type=F32)


def _bmm_nt(a, b):
    return lax.dot_general(a, b, (((2,), (2,)), ((0,), (0,))), preferred_element_type=F32)


def _bmm_tn(a, b):
    return lax.dot_general(a, b, (((1,), (1,)), ((0,), (0,))), preferred_element_type=F32)


def _sigmoid(x):
    return 1.0 / (1.0 + jnp.exp(-x))


def _softplus(x):
    return jnp.maximum(x, 0.0) + jnp.log(1.0 + jnp.exp(-jnp.abs(x)))


def _params(sem):
    return pltpu.CompilerParams(dimension_semantics=sem, vmem_limit_bytes=VMEM_LIMIT)


def _mod_kernel(c_ref, w_ref, b_ref, o_ref):
    c = c_ref[...]
    sc = (c * _sigmoid(c)).astype(BF16)
    o_ref[...] = _dot(sc, w_ref[...].astype(BF16)) + b_ref[...]


def _modulation(c, w_ada, b_ada, tn=1536):
    bsz, d = c.shape
    n = w_ada.shape[1]
    return pl.pallas_call(
        _mod_kernel,
        grid=(n // tn,),
        in_specs=[
            pl.BlockSpec((bsz, d), lambda j: (0, 0)),
            pl.BlockSpec((d, tn), lambda j: (0, j)),
            pl.BlockSpec((1, tn), lambda j: (0, j)),
        ],
        out_specs=pl.BlockSpec((bsz, tn), lambda j: (0, j)),
        out_shape=jax.ShapeDtypeStruct((bsz, n), F32),
        compiler_params=_params(("arbitrary",)),
        name="adaln_mod",
    )(c, w_ada, b_ada.reshape(1, n))


def _inproj_kernel(x_ref, mod_ref, g_ref, w_ref, bd_ref, qg_ref, kg_ref,
                   rw_ref, gate_ref, q_ref, k_ref, v_ref):
    x = x_ref[0]
    ms = jnp.mean(x * x, axis=-1, keepdims=True)
    h = x * lax.rsqrt(ms + RMS_EPS) * g_ref[...]
    h = h * (1.0 + mod_ref[0, 1:2, :]) + mod_ref[0, 0:1, :]
    hb = h.astype(BF16)
    cw = MIX_WIDTH
    for j in range(RWKV_PAD // cw):
        rw_ref[0, :, j * cw:(j + 1) * cw] = _dot(hb, w_ref[:, j * cw:(j + 1) * cw])
    off = RWKV_PAD
    for j in range(2 * x.shape[1] // cw):
        gate_ref[0, :, j * cw:(j + 1) * cw] = _dot(hb, w_ref[:, off + j * cw:off + (j + 1) * cw])
    off += 2 * x.shape[1]

    def head_norm(t, gain):
        msq = _dot((t * t).astype(BF16), bd_ref[...]) * (1.0 / HEAD_DIM)
        return t * lax.rsqrt(msq + RMS_EPS) * gain

    q = head_norm(_dot(hb, w_ref[:, off:off + cw]), qg_ref[...]) * (HEAD_DIM ** -0.5)
    k = head_norm(_dot(hb, w_ref[:, off + cw:off + 2 * cw]), kg_ref[...])
    q_ref[0] = q.astype(BF16)
    k_ref[0] = k.astype(BF16)
    v_ref[0] = _dot(hb, w_ref[:, off + 2 * cw:off + 3 * cw]).astype(BF16)


def _in_projection(x, mod, norm1_g, w_all, bd_ones, qg, kg, tm=512):
    bsz, s, d = x.shape
    n_all = w_all.shape[1]
    head_shape = jax.ShapeDtypeStruct((bsz, s, MIX_WIDTH), BF16)
    head_spec = pl.BlockSpec((1, tm, MIX_WIDTH), lambda b, i: (b, i, 0))
    return pl.pallas_call(
        _inproj_kernel,
        grid=(bsz, s // tm),
        in_specs=[
            pl.BlockSpec((1, tm, d), lambda b, i: (b, i, 0)),
            pl.BlockSpec((1, 6, d), lambda b, i: (b, 0, 0)),
            pl.BlockSpec((1, d), lambda b, i: (0, 0)),
            pl.BlockSpec((d, n_all), lambda b, i: (0, 0), pipeline_mode=pl.Buffered(1)),
            pl.BlockSpec((MIX_WIDTH, MIX_WIDTH), lambda b, i: (0, 0)),
            pl.BlockSpec((1, MIX_WIDTH), lambda b, i: (0, 0)),
            pl.BlockSpec((1, MIX_WIDTH), lambda b, i: (0, 0)),
        ],
        out_specs=[
            pl.BlockSpec((1, tm, RWKV_PAD), lambda b, i: (b, i, 0)),
            pl.BlockSpec((1, tm, 2 * d), lambda b, i: (b, i, 0)),
            head_spec, head_spec, head_spec,
        ],
        out_shape=[
            jax.ShapeDtypeStruct((bsz, s, RWKV_PAD), F32),
            jax.ShapeDtypeStruct((bsz, s, 2 * d), F32),
            head_shape, head_shape, head_shape,
        ],
        compiler_params=_params(("arbitrary", "arbitrary")),
        name="in_proj",
    )(x, mod, norm1_g, w_all, bd_ones, qg, kg)


def _rwkv_kernel(f_ref, mu_ref, w0_ref, a0_ref, kkw_ref, kaw_ref, rk_ref, lng_ref, lnb_ref,
                 wl_ref, wg_ref, bd_ref, cum_ref, tot_ref,
                 y_ref,
                 prev_s, h_s, at_s, rt_s, bt_s, kt_s, bh_s, kh_s, v_s, gc_s, yn_s):
    ts = f_ref.shape[1]
    n_chunks = ts // CHUNK
    w = MIX_WIDTH

    @pl.when(pl.program_id(1) == 0)
    def _():
        prev_s[...] = jnp.zeros_like(prev_s)
        h_s[...] = jnp.zeros_like(h_s)

    f = f_ref[0]
    row = lax.broadcasted_iota(jnp.int32, f.shape, 0)
    shifted = jnp.where(row == 0, prev_s[...], pltpu.roll(f, 1, axis=0))
    prev_s[...] = f[ts - 1:ts, :]
    f = f + (shifted - f) * mu_ref[...]

    r = f[:, 0:w]
    k = f[:, w:2 * w]
    v = f[:, 2 * w:3 * w]
    lora_in = f[:, 3 * w:3 * w + 128]
    gate_in = f[:, 3 * w + 128:3 * w + 384]
    lane = lax.broadcasted_iota(jnp.int32, lora_in.shape, 1)
    lora_act = jnp.where(lane < DECAY_LORA, jnp.tanh(lora_in), lora_in)
    lora = _dot(lora_act.astype(BF16), wl_ref[...])
    g = _dot(_sigmoid(gate_in).astype(BF16), wg_ref[...])

    w_log = -_softplus(-(w0_ref[...] + lora[:, 0:w])) - 0.5
    ld = -jnp.exp(w_log)
    a = _sigmoid(a0_ref[...] + lora[:, w:2 * w])
    kk = k * kkw_ref[...]
    ss = _dot((kk * kk).astype(BF16), bd_ref[...])
    kkn = kk * lax.rsqrt(jnp.maximum(ss, L2_EPS * L2_EPS))
    k2 = k * (1.0 + (a - 1.0) * kaw_ref[...])
    b = kkn * a
    bonus = _dot((r * k2 * rk_ref[...]).astype(BF16), bd_ref[...]) * v

    ld_hi = ld.astype(BF16)
    ld_lo = (ld - ld_hi.astype(F32)).astype(BF16)
    cl = _dot(cum_ref[...], ld_hi) + _dot(cum_ref[...], ld_lo)
    cle = _dot(tot_ref[...], ld_hi) + _dot(tot_ref[...], ld_lo)
    e_neg = jnp.exp(-cl)
    e_end = jnp.exp(cle - cl)
    at_s[...] = (-kkn * jnp.exp(cl - ld)).astype(BF16)
    rt_s[...] = (r * jnp.exp(cl)).astype(BF16)
    bt_s[...] = (b * e_neg).astype(BF16)
    kt_s[...] = (k2 * e_neg).astype(BF16)
    bh_s[...] = (b * e_end).astype(BF16)
    kh_s[...] = (k2 * e_end).astype(BF16)
    v_s[...] = v.astype(BF16)
    gc_s[...] = jnp.exp(cle)

    ri = lax.broadcasted_iota(jnp.int32, (CHUNK, CHUNK), 0)
    ci = lax.broadcasted_iota(jnp.int32, (CHUNK, CHUNK), 1)
    strict = ri > ci
    incl = ri >= ci
    eye = ri == ci

    def blocks(ref):
        return jnp.stack([ref[c * CHUNK:(c + 1) * CHUNK, hd * HEAD_DIM:(hd + 1) * HEAD_DIM]
                          for c in range(n_chunks) for hd in range(N_HEADS)])

    at, rt, bt, kt, bh, kh, vv = (blocks(s) for s in (at_s, rt_s, bt_s, kt_s, bh_s, kh_s, v_s))
    a_ab = jnp.where(strict, _bmm_nt(at, bt), 0.0)
    a_ak = jnp.where(strict, _bmm_nt(at, kt), 0.0).astype(BF16)
    a_rb = jnp.where(incl, _bmm_nt(rt, bt), 0.0).astype(BF16)
    a_rk = jnp.where(incl, _bmm_nt(rt, kt), 0.0).astype(BF16)
    t = jnp.where(eye, 1.0, a_ab)
    p = a_ab
    for _ in range(5):
        pb = p.astype(BF16)
        p = _bmm(pb, pb)
        t = t + _bmm(t.astype(BF16), p.astype(BF16))
    tb = t.astype(BF16)
    p1 = _bmm(tb, at).astype(BF16)
    p2 = _bmm(tb, _bmm(a_ak, vv).astype(BF16)).astype(BF16)
    q1 = (rt.astype(F32) + _bmm(a_rb, p1)).astype(BF16)
    q2 = _bmm(a_rb, p2) + _bmm(a_rk, vv)
    gc = jnp.stack([gc_s[c * CHUNK:c * CHUNK + 1, hd * HEAD_DIM:(hd + 1) * HEAD_DIM]
                    for c in range(n_chunks) for hd in range(N_HEADS)])
    m = (jnp.where(eye, gc, 0.0) + _bmm_tn(bh, p1)).astype(BF16)
    n = _bmm_tn(bh, p2) + _bmm_tn(kh, vv)

    h = h_s[...]
    for c in range(n_chunks):
        sl = slice(c * N_HEADS, (c + 1) * N_HEADS)
        hb = h.astype(BF16)
        y = _bmm(q1[sl], hb) + q2[sl]
        h = _bmm(m[sl], hb) + n[sl]
        mean = jnp.mean(y, axis=-1, keepdims=True)
        yc = y - mean
        var = jnp.mean(yc * yc, axis=-1, keepdims=True)
        yn = yc * lax.rsqrt(var + GN_EPS)
        for hd in range(N_HEADS):
            yn_s[c * CHUNK:(c + 1) * CHUNK, hd * HEAD_DIM:(hd + 1) * HEAD_DIM] = yn[hd]
    h_s[...] = h

    y = (yn_s[...] * lng_ref[...] + lnb_ref[...] + bonus) * g
    y_ref[0] = y.astype(BF16)


def _rwkv_mix(f_rwkv, mu, w0, a0, k_k, k_a, r_k, lnx_g, lnx_b, wl, wg, bd_ones, ts=256):
    bsz, s, fw = f_rwkv.shape
    w = MIX_WIDTH
    ri = jnp.arange(ts)[:, None]
    ci = jnp.arange(ts)[None, :]
    same = (ri // CHUNK) == (ci // CHUNK)
    cum = (same & (ci <= ri)).astype(BF16)
    tot = same.astype(BF16)
    vec = lambda n: pl.BlockSpec((1, n), lambda b, i: (0, 0))
    full = lambda a: pl.BlockSpec(a.shape, lambda b, i: (0, 0))
    return pl.pallas_call(
        _rwkv_kernel,
        grid=(bsz, s // ts),
        in_specs=[
            pl.BlockSpec((1, ts, fw), lambda b, i: (b, i, 0)),
            vec(fw), vec(w), vec(w), vec(w), vec(w), vec(w), vec(w), vec(w),
            full(wl), full(wg), full(bd_ones), full(cum), full(tot),
        ],
        out_specs=pl.BlockSpec((1, ts, w), lambda b, i: (b, i, 0)),
        out_shape=jax.ShapeDtypeStruct((bsz, s, w), BF16),
        scratch_shapes=[
            pltpu.VMEM((1, fw), F32),
            pltpu.VMEM((N_HEADS, HEAD_DIM, HEAD_DIM), F32),
            pltpu.VMEM((ts, w), BF16), pltpu.VMEM((ts, w), BF16), pltpu.VMEM((ts, w), BF16),
            pltpu.VMEM((ts, w), BF16), pltpu.VMEM((ts, w), BF16), pltpu.VMEM((ts, w), BF16),
            pltpu.VMEM((ts, w), BF16),
            pltpu.VMEM((ts, w), F32),
            pltpu.VMEM((ts, w), F32),
        ],
        compiler_params=_params(("arbitrary", "arbitrary")),
        name="rwkv7_mix",
    )(f_rwkv, mu, w0, a0, k_k, k_a, r_k, lnx_g, lnx_b, wl, wg, bd_ones, cum, tot)


def _sb_kernel(q_ref, k_ref, v_ref, suffix_ref, o_ref, z_s, sp_s, acc_s, carry_s):
    tq = q_ref.shape[1]
    tk = suffix_ref.shape[0]
    ratio = tq // tk
    qi = pl.program_id(2)
    suffix = suffix_ref[...]
    lane = lax.broadcasted_iota(jnp.int32, (tq, 2 * HEAD_DIM), 1)
    qp = q_ref[0]
    zero = jnp.zeros_like(qp)
    q_heads = (jnp.where(lane < HEAD_DIM, qp, zero), jnp.where(lane >= HEAD_DIM, qp, zero))

    def key_block(n):
        j = jnp.maximum(qi * ratio + (ratio - 1) - n, 0)
        return j, pl.ds(pl.multiple_of(j * tk, tk), tk)

    def first_row(n):
        return (ratio - 1 - n) * tk if is_own(n) else 0

    def is_own(n):
        return isinstance(n, int) and n < ratio

    def scores(n, slot):
        j, rows = key_block(n)
        r0 = first_row(n)
        kb = k_ref[0, rows, :]
        for hd in range(2):
            z = _dot_nt(q_heads[hd][r0:, :], kb)
            zb = z.astype(BF16)
            sp = jnp.maximum(zb, 0.0) + jnp.log(1.0 + jnp.exp2(jnp.abs(zb) * (-LOG2E)))
            if is_own(n):
                ri = lax.broadcasted_iota(jnp.int32, z.shape, 0)
                ci = lax.broadcasted_iota(jnp.int32, z.shape, 1)
                causal = ci < ri
                sp = jnp.where(causal, sp, jnp.zeros_like(sp))
                z = jnp.where(causal, z, MASKED_SCORE)
            z_s[hd, slot, r0:, :] = z
            sp_s[hd, slot, r0:, :] = sp

    def accumulate(n, slot):
        _, rows = key_block(n)
        r0 = first_row(n)
        vb = v_ref[0, rows, :]
        for hd in range(2):
            cs = _dot(sp_s[hd, slot, r0:, :], suffix)
            carry = carry_s[hd, r0:, :]
            a = jnp.exp(z_s[hd, slot, r0:, :] - cs - jnp.concatenate([carry] * (tk // 128), axis=1))
            acc_s[hd, r0:, :] += _dot(a.astype(BF16), vb)
            carry_s[hd, r0:, :] = carry + jnp.broadcast_to(cs[:, 0:1], carry.shape)

    acc_s[...] = jnp.zeros_like(acc_s)
    carry_s[...] = jnp.zeros_like(carry_s)
    scores(0, 0)
    for n in range(ratio):
        scores(n + 1, (n + 1) % 2)
        accumulate(n, n % 2)

    def group(p, _):
        n0 = ratio + SB_UNROLL * p
        for u in range(SB_UNROLL):
            scores(n0 + u + 1, (u + 1) % 2)
            accumulate(n0 + u, u % 2)
        return 0

    lax.fori_loop(0, qi * (ratio // SB_UNROLL), group, 0)
    o_ref[0] = jnp.where(lane < HEAD_DIM, acc_s[0], acc_s[1]).astype(BF16)


def _sb_attention(q, k, v, tq=1024, tk=256):
    bsz, s, w = q.shape
    pw = 2 * HEAD_DIM
    assert (tq // tk) % SB_UNROLL == 0 and SB_UNROLL % 2 == 0
    suffix = (jnp.arange(tk)[:, None] >= jnp.arange(tk)[None, :]).astype(BF16)
    return pl.pallas_call(
        _sb_kernel,
        grid=(bsz, w // pw, s // tq),
        in_specs=[
            pl.BlockSpec((1, tq, pw), lambda b, h, i: (b, i, h)),
            pl.BlockSpec((1, s, pw), lambda b, h, i: (b, 0, h)),
            pl.BlockSpec((1, s, pw), lambda b, h, i: (b, 0, h)),
            pl.BlockSpec((tk, tk), lambda b, h, i: (0, 0)),
        ],
        out_specs=pl.BlockSpec((1, tq, pw), lambda b, h, i: (b, i, h)),
        out_shape=jax.ShapeDtypeStruct((bsz, s, w), BF16),
        scratch_shapes=[
            pltpu.VMEM((2, 2, tq, tk), F32),
            pltpu.VMEM((2, 2, tq, tk), BF16),
            pltpu.VMEM((2, tq, pw), F32),
            pltpu.VMEM((2, tq, pw), F32),
        ],
        compiler_params=_params(("arbitrary", "arbitrary", "arbitrary")),
        name="stickbreak_attn",
    )(q, k, v, suffix)


def _merge_kernel(x_ref, yr_ref, ys_ref, gate_ref, bg_ref, mod_ref, wor_ref, wos_ref, wout_ref, o_ref):
    d = x_ref.shape[2]
    gr = _sigmoid(gate_ref[0, :, 0:d] + bg_ref[:, 0:d])
    gs = _sigmoid(gate_ref[0, :, d:2 * d] + bg_ref[:, d:2 * d])
    merged = gr * _dot(yr_ref[0], wor_ref[...]) + gs * _dot(ys_ref[0], wos_ref[...])
    o_ref[0] = x_ref[0] + mod_ref[0, 2:3, :] * _dot(merged.astype(BF16), wout_ref[...])


def _merge(x, y_rwkv, y_sb, gates, b_gate, mod, w_or, w_os, w_out, tm=512):
    bsz, s, d = x.shape
    full = lambda a: pl.BlockSpec(a.shape, lambda b, i: (0, 0))
    return pl.pallas_call(
        _merge_kernel,
        grid=(bsz, s // tm),
        in_specs=[
            pl.BlockSpec((1, tm, d), lambda b, i: (b, i, 0)),
            pl.BlockSpec((1, tm, MIX_WIDTH), lambda b, i: (b, i, 0)),
            pl.BlockSpec((1, tm, MIX_WIDTH), lambda b, i: (b, i, 0)),
            pl.BlockSpec((1, tm, 2 * d), lambda b, i: (b, i, 0)),
            pl.BlockSpec((1, 2 * d), lambda b, i: (0, 0)),
            pl.BlockSpec((1, 6, d), lambda b, i: (b, 0, 0)),
            full(w_or), full(w_os), full(w_out),
        ],
        out_specs=pl.BlockSpec((1, tm, d), lambda b, i: (b, i, 0)),
        out_shape=jax.ShapeDtypeStruct((bsz, s, d), F32),
        compiler_params=_params(("arbitrary", "arbitrary")),
        name="merge_out",
    )(x, y_rwkv, y_sb, gates, b_gate, mod, w_or, w_os, w_out)


def _ffn_kernel(x_ref, halo_ref, mod_ref, g_ref, wup_ref, cw_ref, cb_ref, wdn_ref, o_ref, *, tf):
    tm = x_ref.shape[1]
    dff = wdn_ref.shape[0]
    x = x_ref[0]

    def norm_mod(t):
        ms = jnp.mean(t * t, axis=-1, keepdims=True)
        hh = t * lax.rsqrt(ms + RMS_EPS) * g_ref[...]
        return hh * (1.0 + mod_ref[0, 4:5, :]) + mod_ref[0, 3:4, :]

    h = norm_mod(x).astype(BF16)
    keep = (pl.program_id(1) > 0).astype(F32)
    hh = (norm_mod(halo_ref[0]) * keep).astype(BF16)
    hcat = jnp.concatenate([hh, h], axis=0)

    def conv(col0):
        cols = slice(col0, col0 + tf)
        u = _dot(hcat, wup_ref[:, cols])
        cw = cw_ref[:, cols]
        out = cb_ref[:, cols] + cw[2:3, :] * u[CONV_HALO:, :]
        out = out + cw[1:2, :] * u[CONV_HALO - 1:CONV_HALO - 1 + tm, :]
        out = out + cw[0:1, :] * u[CONV_HALO - 2:CONV_HALO - 2 + tm, :]
        return out

    def act(j):
        val = conv(j * tf)
        gt = conv(dff + j * tf)
        return (gt * _sigmoid(gt) * val).astype(BF16)

    n_tiles = dff // tf
    acc = jnp.zeros((tm, x.shape[1]), F32)
    nxt = act(0)
    for j in range(n_tiles):
        cur = nxt
        if j + 1 < n_tiles:
            nxt = act(j + 1)
        acc = acc + _dot(cur, wdn_ref[j * tf:(j + 1) * tf, :])
    o_ref[0] = x + mod_ref[0, 5:6, :] * acc


def _ffn(x, mod, norm2_g, w_up, conv_w, conv_b, w_down, tm=512, tf=MXU_TILE):
    bsz, s, d = x.shape
    dff = w_down.shape[0]
    pad = -dff % tf
    halves = lambda a: jnp.concatenate(
        [jnp.pad(a[..., :dff], ((0, 0), (0, pad))), jnp.pad(a[..., dff:], ((0, 0), (0, pad)))], axis=-1)
    w_up = halves(w_up).astype(BF16)
    conv_w = halves(conv_w)
    conv_b = halves(conv_b)
    w_down = jnp.pad(w_down, ((0, pad), (0, 0))).astype(BF16)
    hb = tm // CONV_HALO
    full = lambda a: pl.BlockSpec(a.shape, lambda b, i: (0, 0), pipeline_mode=pl.Buffered(1))
    return pl.pallas_call(
        functools.partial(_ffn_kernel, tf=tf),
        grid=(bsz, s // tm),
        in_specs=[
            pl.BlockSpec((1, tm, d), lambda b, i: (b, i, 0)),
            pl.BlockSpec((1, CONV_HALO, d), lambda b, i: (b, jnp.maximum(i * hb - 1, 0), 0)),
            pl.BlockSpec((1, 6, d), lambda b, i: (b, 0, 0)),
            pl.BlockSpec((1, d), lambda b, i: (0, 0)),
            full(w_up), full(conv_w), full(conv_b), full(w_down),
        ],
        out_specs=pl.BlockSpec((1, tm, d), lambda b, i: (b, i, 0)),
        out_shape=jax.ShapeDtypeStruct((bsz, s, d), F32),
        compiler_params=_params(("arbitrary", "arbitrary")),
        name="conv_ffn",
    )(x, x, mod, norm2_g, w_up, conv_w, conv_b, w_down)


def _layer(x, c, w_ada, b_ada, norm1_g, w_in, b_gate, rwkv_mu, rwkv_w0, rwkv_w2, rwkv_a0,
           rwkv_a2, rwkv_g2, rwkv_k_k, rwkv_k_a, rwkv_r_k, rwkv_lnx_g, rwkv_lnx_b, sb_q_g,
           sb_k_g, w_o_rwkv, w_o_sb, w_out, norm2_g, w_up, conv_w, conv_b, w_down):
    bsz, s, d = x.shape
    w = MIX_WIDTH
    rwkv_in = 3 * w + DECAY_LORA + ICLR_LORA + GATE_LORA
    sb_in = 3 * w
    pad = RWKV_PAD - rwkv_in

    w_all = jnp.concatenate([
        w_in[:, :rwkv_in], jnp.zeros((d, pad), F32),
        w_in[:, rwkv_in + sb_in:], w_in[:, rwkv_in:rwkv_in + sb_in]], axis=1).astype(BF16)
    mu = jnp.concatenate([rwkv_mu, jnp.zeros((pad,), F32)]).reshape(1, RWKV_PAD)
    wl = jnp.zeros((DECAY_LORA + ICLR_LORA, 2 * w), F32)
    wl = wl.at[:DECAY_LORA, :w].set(rwkv_w2).at[DECAY_LORA:, w:].set(rwkv_a2).astype(BF16)
    wg = jnp.zeros((256, w), F32).at[:GATE_LORA].set(rwkv_g2).astype(BF16)
    head_id = jnp.arange(w) // HEAD_DIM
    bd_ones = (head_id[:, None] == head_id[None, :]).astype(BF16)
    row = lambda a: a.reshape(1, -1)

    mod = _modulation(c, w_ada, b_ada).reshape(bsz, 6, d)
    f_rwkv, gates, q, k, v = _in_projection(
        x, mod, row(norm1_g), w_all, bd_ones,
        row(jnp.tile(sb_q_g, N_HEADS)), row(jnp.tile(sb_k_g, N_HEADS)))
    y_rwkv = _rwkv_mix(f_rwkv, mu, row(rwkv_w0), row(rwkv_a0), row(rwkv_k_k), row(rwkv_k_a),
                       row(rwkv_r_k), row(rwkv_lnx_g), row(rwkv_lnx_b), wl, wg, bd_ones)
    y_sb = _sb_attention(q, k, v)
    x = _merge(x, y_rwkv, y_sb, gates, row(b_gate), mod,
               w_o_rwkv.astype(BF16), w_o_sb.astype(BF16), w_out.astype(BF16))
    return _ffn(x, mod, row(norm2_g), w_up, conv_w, row(conv_b), w_down)


def kernel(x, c, w_ada, b_ada, norm1_g, w_in, b_gate, rwkv_mu, rwkv_w0, rwkv_w2, rwkv_a0, rwkv_a2, rwkv_g2, rwkv_k_k, rwkv_k_a, rwkv_r_k, rwkv_lnx_g, rwkv_lnx_b, sb_q_g, sb_k_g, w_o_rwkv, w_o_sb, w_out, norm2_g, w_up, conv_w, conv_b, w_down):
    params = (w_ada, b_ada, norm1_g, w_in, b_gate, rwkv_mu, rwkv_w0, rwkv_w2, rwkv_a0, rwkv_a2,
              rwkv_g2, rwkv_k_k, rwkv_k_a, rwkv_r_k, rwkv_lnx_g, rwkv_lnx_b, sb_q_g, sb_k_g,
              w_o_rwkv, w_o_sb, w_out, norm2_g, w_up, conv_w, conv_b, w_down)
    for layer in range(w_ada.shape[0]):
        x = _layer(x, c, *(p[layer] for p in params))
    return x
```

```python
import functools

import jax
import jax.numpy as jnp
from jax import lax
from jax.experimental import pallas as pl
from jax.experimental.pallas import tpu as pltpu

F32 = jnp.float32
BF16 = jnp.bfloat16

HEAD_DIM = 64
N_HEADS = 8
MIX_WIDTH = HEAD_DIM * N_HEADS
DECAY_LORA = 64
ICLR_LORA = 64
GATE_LORA = 160
RWKV_PAD = 2048
CONV_WIDTH = 3
CONV_HALO = 8
RMS_EPS = 1e-6
GN_EPS = 64e-5
L2_EPS = 1e-12
CHUNK = 64
LOG2E = 1.4426950408889634
MASKED_SCORE = -1e30
SB_UNROLL = 4
MXU_TILE = 256

VMEM_LIMIT = 56 * 1024 * 1024


def _dot(a, b):
    return jnp.dot(a, b, preferred_element_type=F32)


def _dot_nt(a, b):
    return lax.dot_general(a, b, (((1,), (1,)), ((), ())), preferred_element_type=F32)


def _bmm(a, b):
    return lax.dot_general(a, b, (((2,), (1,)), ((0,), (0,))), preferred_element_type=F32)


def _bmm_nt(a, b):
    return lax.dot_general(a, b, (((2,), (2,)), ((0,), (0,))), preferred_element_type=F32)


def _bmm_tn(a, b):
    return lax.dot_general(a, b, (((1,), (1,)), ((0,), (0,))), preferred_element_type=F32)


def _sigmoid(x):
    return 1.0 / (1.0 + jnp.exp(-x))


def _softplus(x):
    return jnp.maximum(x, 0.0) + jnp.log(1.0 + jnp.exp(-jnp.abs(x)))


def _params(sem):
    return pltpu.CompilerParams(dimension_semantics=sem, vmem_limit_bytes=VMEM_LIMIT)


def _mod_kernel(c_ref, w_ref, b_ref, o_ref):
    c = c_ref[...]
    sc = (c * _sigmoid(c)).astype(BF16)
    o_ref[...] = _dot(sc, w_ref[...].astype(BF16)) + b_ref[...]


def _modulation(c, w_ada, b_ada, tn=1536):
    bsz, d = c.shape
    n = w_ada.shape[1]
    return pl.pallas_call(
        _mod_kernel,
        grid=(n // tn,),
        in_specs=[
            pl.BlockSpec((bsz, d), lambda j: (0, 0)),
            pl.BlockSpec((d, tn), lambda j: (0, j)),
            pl.BlockSpec((1, tn), lambda j: (0, j)),
        ],
        out_specs=pl.BlockSpec((bsz, tn), lambda j: (0, j)),
        out_shape=jax.ShapeDtypeStruct((bsz, n), F32),
        compiler_params=_params(("arbitrary",)),
        name="adaln_mod",
    )(c, w_ada, b_ada.reshape(1, n))


def _inproj_kernel(x_ref, mod_ref, g_ref, w_ref, bd_ref, qg_ref, kg_ref,
                   rw_ref, gate_ref, q_ref, k_ref, v_ref):
    x = x_ref[0]
    ms = jnp.mean(x * x, axis=-1, keepdims=True)
    h = x * lax.rsqrt(ms + RMS_EPS) * g_ref[...]
    h = h * (1.0 + mod_ref[0, 1:2, :]) + mod_ref[0, 0:1, :]
    hb = h.astype(BF16)
    cw = MIX_WIDTH
    for j in range(RWKV_PAD // cw):
        rw_ref[0, :, j * cw:(j + 1) * cw] = _dot(hb, w_ref[:, j * cw:(j + 1) * cw])
    off = RWKV_PAD
    for j in range(2 * x.shape[1] // cw):
        gate_ref[0, :, j * cw:(j + 1) * cw] = _dot(hb, w_ref[:, off + j * cw:off + (j + 1) * cw])
    off += 2 * x.shape[1]

    def head_norm(t, gain):
        msq = _dot((t * t).astype(BF16), bd_ref[...]) * (1.0 / HEAD_DIM)
        return t * lax.rsqrt(msq + RMS_EPS) * gain

    q = head_norm(_dot(hb, w_ref[:, off:off + cw]), qg_ref[...]) * (HEAD_DIM ** -0.5)
    k = head_norm(_dot(hb, w_ref[:, off + cw:off + 2 * cw]), kg_ref[...])
    q_ref[0] = q.astype(BF16)
    k_ref[0] = k.astype(BF16)
    v_ref[0] = _dot(hb, w_ref[:, off + 2 * cw:off + 3 * cw]).astype(BF16)


def _in_projection(x, mod, norm1_g, w_all, bd_ones, qg, kg, tm=512):
    bsz, s, d = x.shape
    n_all = w_all.shape[1]
    head_shape = jax.ShapeDtypeStruct((bsz, s, MIX_WIDTH), BF16)
    head_spec = pl.BlockSpec((1, tm, MIX_WIDTH), lambda b, i: (b, i, 0))
    return pl.pallas_call(
        _inproj_kernel,
        grid=(bsz, s // tm),
        in_specs=[
            pl.BlockSpec((1, tm, d), lambda b, i: (b, i, 0)),
            pl.BlockSpec((1, 6, d), lambda b, i: (b, 0, 0)),
            pl.BlockSpec((1, d), lambda b, i: (0, 0)),
            pl.BlockSpec((d, n_all), lambda b, i: (0, 0), pipeline_mode=pl.Buffered(1)),
            pl.BlockSpec((MIX_WIDTH, MIX_WIDTH), lambda b, i: (0, 0)),
            pl.BlockSpec((1, MIX_WIDTH), lambda b, i: (0, 0)),
            pl.BlockSpec((1, MIX_WIDTH), lambda b, i: (0, 0)),
        ],
        out_specs=[
            pl.BlockSpec((1, tm, RWKV_PAD), lambda b, i: (b, i, 0)),
            pl.BlockSpec((1, tm, 2 * d), lambda b, i: (b, i, 0)),
            head_spec, head_spec, head_spec,
        ],
        out_shape=[
            jax.ShapeDtypeStruct((bsz, s, RWKV_PAD), F32),
            jax.ShapeDtypeStruct((bsz, s, 2 * d), F32),
            head_shape, head_shape, head_shape,
        ],
        compiler_params=_params(("arbitrary", "arbitrary")),
        name="in_proj",
    )(x, mod, norm1_g, w_all, bd_ones, qg, kg)


def _rwkv_kernel(f_ref, mu_ref, w0_ref, a0_ref, kkw_ref, kaw_ref, rk_ref, lng_ref, lnb_ref,
                 wl_ref, wg_ref, bd_ref, cum_ref, tot_ref,
                 y_ref,
                 prev_s, h_s, at_s, rt_s, bt_s, kt_s, bh_s, kh_s, v_s, gc_s, yn_s):
    ts = f_ref.shape[1]
    n_chunks = ts // CHUNK
    w = MIX_WIDTH

    @pl.when(pl.program_id(1) == 0)
    def _():
        prev_s[...] = jnp.zeros_like(prev_s)
        h_s[...] = jnp.zeros_like(h_s)

    f = f_ref[0]
    row = lax.broadcasted_iota(jnp.int32, f.shape, 0)
    shifted = jnp.where(row == 0, prev_s[...], pltpu.roll(f, 1, axis=0))
    prev_s[...] = f[ts - 1:ts, :]
    f = f + (shifted - f) * mu_ref[...]

    r = f[:, 0:w]
    k = f[:, w:2 * w]
    v = f[:, 2 * w:3 * w]
    lora_in = f[:, 3 * w:3 * w + 128]
    gate_in = f[:, 3 * w + 128:3 * w + 384]
    lane = lax.broadcasted_iota(jnp.int32, lora_in.shape, 1)
    lora_act = jnp.where(lane < DECAY_LORA, jnp.tanh(lora_in), lora_in)
    lora = _dot(lora_act.astype(BF16), wl_ref[...])
    g = _dot(_sigmoid(gate_in).astype(BF16), wg_ref[...])

    w_log = -_softplus(-(w0_ref[...] + lora[:, 0:w])) - 0.5
    ld = -jnp.exp(w_log)
    a = _sigmoid(a0_ref[...] + lora[:, w:2 * w])
    kk = k * kkw_ref[...]
    ss = _dot((kk * kk).astype(BF16), bd_ref[...])
    kkn = kk * lax.rsqrt(jnp.maximum(ss, L2_EPS * L2_EPS))
    k2 = k * (1.0 + (a - 1.0) * kaw_ref[...])
    b = kkn * a
    bonus = _dot((r * k2 * rk_ref[...]).astype(BF16), bd_ref[...]) * v

    ld_hi = ld.astype(BF16)
    ld_lo = (ld - ld_hi.astype(F32)).astype(BF16)
    cl = _dot(cum_ref[...], ld_hi) + _dot(cum_ref[...], ld_lo)
    cle = _dot(tot_ref[...], ld_hi) + _dot(tot_ref[...], ld_lo)
    e_neg = jnp.exp(-cl)
    e_end = jnp.exp(cle - cl)
    at_s[...] = (-kkn * jnp.exp(cl - ld)).astype(BF16)
    rt_s[...] = (r * jnp.exp(cl)).astype(BF16)
    bt_s[...] = (b * e_neg).astype(BF16)
    kt_s[...] = (k2 * e_neg).astype(BF16)
    bh_s[...] = (b * e_end).astype(BF16)
    kh_s[...] = (k2 * e_end).astype(BF16)
    v_s[...] = v.astype(BF16)
    gc_s[...] = jnp.exp(cle)

    ri = lax.broadcasted_iota(jnp.int32, (CHUNK, CHUNK), 0)
    ci = lax.broadcasted_iota(jnp.int32, (CHUNK, CHUNK), 1)
    strict = ri > ci
    incl = ri >= ci
    eye = ri == ci

    def blocks(ref):
        return jnp.stack([ref[c * CHUNK:(c + 1) * CHUNK, hd * HEAD_DIM:(hd + 1) * HEAD_DIM]
                          for c in range(n_chunks) for hd in range(N_HEADS)])

    at, rt, bt, kt, bh, kh, vv = (blocks(s) for s in (at_s, rt_s, bt_s, kt_s, bh_s, kh_s, v_s))
    a_ab = jnp.where(strict, _bmm_nt(at, bt), 0.0)
    a_ak = jnp.where(strict, _bmm_nt(at, kt), 0.0).astype(BF16)
    a_rb = jnp.where(incl, _bmm_nt(rt, bt), 0.0).astype(BF16)
    a_rk = jnp.where(incl, _bmm_nt(rt, kt), 0.0).astype(BF16)
    t = jnp.where(eye, 1.0, a_ab)
    p = a_ab
    for _ in range(5):
        pb = p.astype(BF16)
        p = _bmm(pb, pb)
        t = t + _bmm(t.astype(BF16), p.astype(BF16))
    tb = t.astype(BF16)
    p1 = _bmm(tb, at).astype(BF16)
    p2 = _bmm(tb, _bmm(a_ak, vv).astype(BF16)).astype(BF16)
    q1 = (rt.astype(F32) + _bmm(a_rb, p1)).astype(BF16)
    q2 = _bmm(a_rb, p2) + _bmm(a_rk, vv)
    gc = jnp.stack([gc_s[c * CHUNK:c * CHUNK + 1, hd * HEAD_DIM:(hd + 1) * HEAD_DIM]
                    for c in range(n_chunks) for hd in range(N_HEADS)])
    m = (jnp.where(eye, gc, 0.0) + _bmm_tn(bh, p1)).astype(BF16)
    n = _bmm_tn(bh, p2) + _bmm_tn(kh, vv)

    h = h_s[...]
    for c in range(n_chunks):
        sl = slice(c * N_HEADS, (c + 1) * N_HEADS)
        hb = h.astype(BF16)
        y = _bmm(q1[sl], hb) + q2[sl]
        h = _bmm(m[sl], hb) + n[sl]
        mean = jnp.mean(y, axis=-1, keepdims=True)
        yc = y - mean
        var = jnp.mean(yc * yc, axis=-1, keepdims=True)
        yn = yc * lax.rsqrt(var + GN_EPS)
        for hd in range(N_HEADS):
            yn_s[c * CHUNK:(c + 1) * CHUNK, hd * HEAD_DIM:(hd + 1) * HEAD_DIM] = yn[hd]
    h_s[...] = h

    y = (yn_s[...] * lng_ref[...] + lnb_ref[...] + bonus) * g
    y_ref[0] = y.astype(BF16)


def _sb_kernel(qi, q_ref, k_ref, v_ref, suffix_ref, o_ref, z_s, sp_s, acc_s, carry_s):
    tq = q_ref.shape[1]
    tk = suffix_ref.shape[0]
    ratio = tq // tk
    suffix = suffix_ref[...]
    lane = lax.broadcasted_iota(jnp.int32, (tq, 2 * HEAD_DIM), 1)
    qp = q_ref[0]
    zero = jnp.zeros_like(qp)
    q_heads = (jnp.where(lane < HEAD_DIM, qp, zero), jnp.where(lane >= HEAD_DIM, qp, zero))

    def key_block(n):
        j = jnp.maximum(qi * ratio + (ratio - 1) - n, 0)
        return j, pl.ds(pl.multiple_of(j * tk, tk), tk)

    def first_row(n):
        return (ratio - 1 - n) * tk if is_own(n) else 0

    def is_own(n):
        return isinstance(n, int) and n < ratio

    def scores(n, slot):
        j, rows = key_block(n)
        r0 = first_row(n)
        kb = k_ref[0, rows, :]
        for hd in range(2):
            z = _dot_nt(q_heads[hd][r0:, :], kb)
            zb = z.astype(BF16)
            sp = jnp.maximum(zb, 0.0) + jnp.log(1.0 + jnp.exp2(jnp.abs(zb) * (-LOG2E)))
            if is_own(n):
                ri = lax.broadcasted_iota(jnp.int32, z.shape, 0)
                ci = lax.broadcasted_iota(jnp.int32, z.shape, 1)
                causal = ci < ri
                sp = jnp.where(causal, sp, jnp.zeros_like(sp))
                z = jnp.where(causal, z, MASKED_SCORE)
            z_s[hd, slot, r0:, :] = z
            sp_s[hd, slot, r0:, :] = sp

    def accumulate(n, slot):
        _, rows = key_block(n)
        r0 = first_row(n)
        vb = v_ref[0, rows, :]
        for hd in range(2):
            cs = _dot(sp_s[hd, slot, r0:, :], suffix)
            carry = carry_s[hd, r0:, :]
            a = jnp.exp(z_s[hd, slot, r0:, :] - cs - jnp.concatenate([carry] * (tk // 128), axis=1))
            acc_s[hd, r0:, :] += _dot(a.astype(BF16), vb)
            carry_s[hd, r0:, :] = carry + jnp.broadcast_to(cs[:, 0:1], carry.shape)

    acc_s[...] = jnp.zeros_like(acc_s)
    carry_s[...] = jnp.zeros_like(carry_s)
    scores(0, 0)
    for n in range(ratio):
        scores(n + 1, (n + 1) % 2)
        accumulate(n, n % 2)

    def group(p, _):
        n0 = ratio + SB_UNROLL * p
        for u in range(SB_UNROLL):
            scores(n0 + u + 1, (u + 1) % 2)
            accumulate(n0 + u, u % 2)
        return 0

    lax.fori_loop(0, qi * (ratio // SB_UNROLL), group, 0)
    o_ref[0] = jnp.where(lane < HEAD_DIM, acc_s[0], acc_s[1]).astype(BF16)


N_RWKV_IN = 14
N_RWKV_SCRATCH = 11


def _mix_kernel(*refs):
    rw_in, sb_in = refs[:N_RWKV_IN], refs[N_RWKV_IN:N_RWKV_IN + 4]
    y_ref, o_ref = refs[N_RWKV_IN + 4:N_RWKV_IN + 6]
    scratch = refs[N_RWKV_IN + 6:]
    n_q = sb_in[1].shape[1] // sb_in[0].shape[1]
    _rwkv_kernel(*rw_in, y_ref, *scratch[:N_RWKV_SCRATCH])
    _sb_kernel(pl.program_id(1) % n_q, *sb_in, o_ref, *scratch[N_RWKV_SCRATCH:])


def _token_mixers(f_rwkv, mu, w0, a0, k_k, k_a, r_k, lnx_g, lnx_b, wl, wg, bd_ones, q, k, v,
                  ts=256, tq=1024, tk=256):
    bsz, s, fw = f_rwkv.shape
    w = MIX_WIDTH
    pw = 2 * HEAD_DIM
    n_q = s // tq
    assert s // ts == (w // pw) * n_q
    assert (tq // tk) % SB_UNROLL == 0 and SB_UNROLL % 2 == 0
    ri = jnp.arange(ts)[:, None]
    ci = jnp.arange(ts)[None, :]
    same = (ri // CHUNK) == (ci // CHUNK)
    cum = (same & (ci <= ri)).astype(BF16)
    tot = same.astype(BF16)
    suffix = (jnp.arange(tk)[:, None] >= jnp.arange(tk)[None, :]).astype(BF16)
    vec = lambda n: pl.BlockSpec((1, n), lambda b, i: (0, 0))
    full = lambda a: pl.BlockSpec(a.shape, lambda b, i: (0, 0))
    return pl.pallas_call(
        _mix_kernel,
        grid=(bsz, s // ts),
        in_specs=[
            pl.BlockSpec((1, ts, fw), lambda b, i: (b, i, 0)),
            vec(fw), vec(w), vec(w), vec(w), vec(w), vec(w), vec(w), vec(w),
            full(wl), full(wg), full(bd_ones), full(cum), full(tot),
            pl.BlockSpec((1, tq, pw), lambda b, i: (b, i % n_q, i // n_q)),
            pl.BlockSpec((1, s, pw), lambda b, i: (b, 0, i // n_q)),
            pl.BlockSpec((1, s, pw), lambda b, i: (b, 0, i // n_q)),
            pl.BlockSpec((tk, tk), lambda b, i: (0, 0)),
        ],
        out_specs=[
            pl.BlockSpec((1, ts, w), lambda b, i: (b, i, 0)),
            pl.BlockSpec((1, tq, pw), lambda b, i: (b, i % n_q, i // n_q)),
        ],
        out_shape=[jax.ShapeDtypeStruct((bsz, s, w), BF16), jax.ShapeDtypeStruct((bsz, s, w), BF16)],
        scratch_shapes=[
            pltpu.VMEM((1, fw), F32),
            pltpu.VMEM((N_HEADS, HEAD_DIM, HEAD_DIM), F32),
            pltpu.VMEM((ts, w), BF16), pltpu.VMEM((ts, w), BF16), pltpu.VMEM((ts, w), BF16),
            pltpu.VMEM((ts, w), BF16), pltpu.VMEM((ts, w), BF16), pltpu.VMEM((ts, w), BF16),
            pltpu.VMEM((ts, w), BF16),
            pltpu.VMEM((ts, w), F32),
            pltpu.VMEM((ts, w), F32),
            pltpu.VMEM((2, 2, tq, tk), F32),
            pltpu.VMEM((2, 2, tq, tk), BF16),
            pltpu.VMEM((2, tq, pw), F32),
            pltpu.VMEM((2, tq, pw), F32),
        ],
        compiler_params=_params(("arbitrary", "arbitrary")),
        name="token_mixers",
    )(f_rwkv, mu, w0, a0, k_k, k_a, r_k, lnx_g, lnx_b, wl, wg, bd_ones, cum, tot, q, k, v, suffix)


def _merge_kernel(x_ref, yr_ref, ys_ref, gate_ref, bg_ref, mod_ref, wor_ref, wos_ref, wout_ref, o_ref):
    d = x_ref.shape[2]
    gr = _sigmoid(gate_ref[0, :, 0:d] + bg_ref[:, 0:d])
    gs = _sigmoid(gate_ref[0, :, d:2 * d] + bg_ref[:, d:2 * d])
    merged = gr * _dot(yr_ref[0], wor_ref[...]) + gs * _dot(ys_ref[0], wos_ref[...])
    o_ref[0] = x_ref[0] + mod_ref[0, 2:3, :] * _dot(merged.astype(BF16), wout_ref[...])


def _merge(x, y_rwkv, y_sb, gates, b_gate, mod, w_or, w_os, w_out, tm=512):
    bsz, s, d = x.shape
    full = lambda a: pl.BlockSpec(a.shape, lambda b, i: (0, 0))
    return pl.pallas_call(
        _merge_kernel,
        grid=(bsz, s // tm),
        in_specs=[
            pl.BlockSpec((1, tm, d), lambda b, i: (b, i, 0)),
            pl.BlockSpec((1, tm, MIX_WIDTH), lambda b, i: (b, i, 0)),
            pl.BlockSpec((1, tm, MIX_WIDTH), lambda b, i: (b, i, 0)),
            pl.BlockSpec((1, tm, 2 * d), lambda b, i: (b, i, 0)),
            pl.BlockSpec((1, 2 * d), lambda b, i: (0, 0)),
            pl.BlockSpec((1, 6, d), lambda b, i: (b, 0, 0)),
            full(w_or), full(w_os), full(w_out),
        ],
        out_specs=pl.BlockSpec((1, tm, d), lambda b, i: (b, i, 0)),
        out_shape=jax.ShapeDtypeStruct((bsz, s, d), F32),
        compiler_params=_params(("arbitrary", "arbitrary")),
        name="merge_out",
    )(x, y_rwkv, y_sb, gates, b_gate, mod, w_or, w_os, w_out)


def _ffn_kernel(x_ref, halo_ref, mod_ref, g_ref, wup_ref, cw_ref, cb_ref, wdn_ref, o_ref, *u_bufs, tf):
    tm = x_ref.shape[1]
    dff = wdn_ref.shape[0]
    x = x_ref[0]

    def norm_mod(t):
        ms = jnp.mean(t * t, axis=-1, keepdims=True)
        hh = t * lax.rsqrt(ms + RMS_EPS) * g_ref[...]
        return hh * (1.0 + mod_ref[0, 4:5, :]) + mod_ref[0, 3:4, :]

    h = norm_mod(x).astype(BF16)
    keep = (pl.program_id(1) > 0).astype(F32)
    hh = (norm_mod(halo_ref[0]) * keep).astype(BF16)
    hcat = jnp.concatenate([hh, h], axis=0)

    def conv(col0):
        cols = slice(col0, col0 + tf)
        u_s = u_bufs[(col0 // tf) % len(u_bufs)]
        u_s[...] = _dot(hcat, wup_ref[:, cols])
        cw = cw_ref[:, cols]
        out = cb_ref[:, cols] + cw[2:3, :] * u_s[CONV_HALO:, :]
        out = out + cw[1:2, :] * u_s[CONV_HALO - 1:CONV_HALO - 1 + tm, :]
        out = out + cw[0:1, :] * u_s[CONV_HALO - 2:CONV_HALO - 2 + tm, :]
        return out

    def act(j):
        val = conv(j * tf)
        gt = conv(dff + j * tf)
        return (gt * _sigmoid(gt) * val).astype(BF16)

    n_tiles = dff // tf
    acc = jnp.zeros((tm, x.shape[1]), F32)
    nxt = act(0)
    for j in range(n_tiles):
        cur = nxt
        if j + 1 < n_tiles:
            nxt = act(j + 1)
        acc = acc + _dot(cur, wdn_ref[j * tf:(j + 1) * tf, :])
    o_ref[0] = x + mod_ref[0, 5:6, :] * acc


def _ffn(x, mod, norm2_g, w_up, conv_w, conv_b, w_down, tm=512, tf=MXU_TILE):
    bsz, s, d = x.shape
    dff = w_down.shape[0]
    pad = -dff % tf
    halves = lambda a: jnp.concatenate(
        [jnp.pad(a[..., :dff], ((0, 0), (0, pad))), jnp.pad(a[..., dff:], ((0, 0), (0, pad)))], axis=-1)
    w_up = halves(w_up).astype(BF16)
    conv_w = halves(conv_w)
    conv_b = halves(conv_b)
    w_down = jnp.pad(w_down, ((0, pad), (0, 0))).astype(BF16)
    hb = tm // CONV_HALO
    full = lambda a: pl.BlockSpec(a.shape, lambda b, i: (0, 0), pipeline_mode=pl.Buffered(1))
    return pl.pallas_call(
        functools.partial(_ffn_kernel, tf=tf),
        grid=(bsz, s // tm),
        in_specs=[
            pl.BlockSpec((1, tm, d), lambda b, i: (b, i, 0)),
            pl.BlockSpec((1, CONV_HALO, d), lambda b, i: (b, jnp.maximum(i * hb - 1, 0), 0)),
            pl.BlockSpec((1, 6, d), lambda b, i: (b, 0, 0)),
            pl.BlockSpec((1, d), lambda b, i: (0, 0)),
            full(w_up), full(conv_w), full(conv_b), full(w_down),
        ],
        out_specs=pl.BlockSpec((1, tm, d), lambda b, i: (b, i, 0)),
        out_shape=jax.ShapeDtypeStruct((bsz, s, d), F32),
        scratch_shapes=[pltpu.VMEM((CONV_HALO + tm, tf), F32) for _ in range(4)],
        compiler_params=_params(("arbitrary", "arbitrary")),
        name="conv_ffn",
    )(x, x, mod, norm2_g, w_up, conv_w, conv_b, w_down)


def _layer(x, c, w_ada, b_ada, norm1_g, w_in, b_gate, rwkv_mu, rwkv_w0, rwkv_w2, rwkv_a0,
           rwkv_a2, rwkv_g2, rwkv_k_k, rwkv_k_a, rwkv_r_k, rwkv_lnx_g, rwkv_lnx_b, sb_q_g,
           sb_k_g, w_o_rwkv, w_o_sb, w_out, norm2_g, w_up, conv_w, conv_b, w_down):
    bsz, s, d = x.shape
    w = MIX_WIDTH
    rwkv_in = 3 * w + DECAY_LORA + ICLR_LORA + GATE_LORA
    sb_in = 3 * w
    pad = RWKV_PAD - rwkv_in

    w_all = jnp.concatenate([
        w_in[:, :rwkv_in], jnp.zeros((d, pad), F32),
        w_in[:, rwkv_in + sb_in:], w_in[:, rwkv_in:rwkv_in + sb_in]], axis=1).astype(BF16)
    mu = jnp.concatenate([rwkv_mu, jnp.zeros((pad,), F32)]).reshape(1, RWKV_PAD)
    wl = jnp.zeros((DECAY_LORA + ICLR_LORA, 2 * w), F32)
    wl = wl.at[:DECAY_LORA, :w].set(rwkv_w2).at[DECAY_LORA:, w:].set(rwkv_a2).astype(BF16)
    wg = jnp.zeros((256, w), F32).at[:GATE_LORA].set(rwkv_g2).astype(BF16)
    head_id = jnp.arange(w) // HEAD_DIM
    bd_ones = (head_id[:, None] == head_id[None, :]).astype(BF16)
    row = lambda a: a.reshape(1, -1)

    mod = _modulation(c, w_ada, b_ada).reshape(bsz, 6, d)
    f_rwkv, gates, q, k, v = _in_projection(
        x, mod, row(norm1_g), w_all, bd_ones,
        row(jnp.tile(sb_q_g, N_HEADS)), row(jnp.tile(sb_k_g, N_HEADS)))
    y_rwkv, y_sb = _token_mixers(
        f_rwkv, mu, row(rwkv_w0), row(rwkv_a0), row(rwkv_k_k), row(rwkv_k_a), row(rwkv_r_k),
        row(rwkv_lnx_g), row(rwkv_lnx_b), wl, wg, bd_ones, q, k, v)
    x = _merge(x, y_rwkv, y_sb, gates, row(b_gate), mod,
               w_o_rwkv.astype(BF16), w_o_sb.astype(BF16), w_out.astype(BF16))
    return _ffn(x, mod, row(norm2_g), w_up, conv_w, row(conv_b), w_down)


def kernel(x, c, w_ada, b_ada, norm1_g, w_in, b_gate, rwkv_mu, rwkv_w0, rwkv_w2, rwkv_a0, rwkv_a2, rwkv_g2, rwkv_k_k, rwkv_k_a, rwkv_r_k, rwkv_lnx_g, rwkv_lnx_b, sb_q_g, sb_k_g, w_o_rwkv, w_o_sb, w_out, norm2_g, w_up, conv_w, conv_b, w_down):
    params = (w_ada, b_ada, norm1_g, w_in, b_gate, rwkv_mu, rwkv_w0, rwkv_w2, rwkv_a0, rwkv_a2,
              rwkv_g2, rwkv_k_k, rwkv_k_a, rwkv_r_k, rwkv_lnx_g, rwkv_lnx_b, sb_q_g, sb_k_g,
              w_o_rwkv, w_o_sb, w_out, norm2_g, w_up, conv_w, conv_b, w_down)
    for layer in range(w_ada.shape[0]):
        x = _layer(x, c, *(p[layer] for p in params))
    return x
```

```python
import functools

import jax
import jax.numpy as jnp
from jax import lax
from jax.experimental import pallas as pl
from jax.experimental.pallas import tpu as pltpu

F32 = jnp.float32
BF16 = jnp.bfloat16

HEAD_DIM = 64
N_HEADS = 8
MIX_WIDTH = HEAD_DIM * N_HEADS
DECAY_LORA = 64
ICLR_LORA = 64
GATE_LORA = 160
RWKV_PAD = 2048
CONV_WIDTH = 3
CONV_HALO = 8
RMS_EPS = 1e-6
GN_EPS = 64e-5
L2_EPS = 1e-12
CHUNK = 64
LOG2E = 1.4426950408889634
MASKED_SCORE = -1e30
SB_UNROLL = 4
MXU_TILE = 256

VMEM_LIMIT = 56 * 1024 * 1024


def _dot(a, b):
    return jnp.dot(a, b, preferred_element_type=F32)


def _dot_nt(a, b):
    return lax.dot_general(a, b, (((1,), (1,)), ((), ())), preferred_element_type=F32)


def _bmm(a, b):
    return lax.dot_general(a, b, (((2,), (1,)), ((0,), (0,))), preferred_element_type=F32)


def _bmm_nt(a, b):
    return lax.dot_general(a, b, (((2,), (2,)), ((0,), (0,))), preferred_element_type=F32)


def _bmm_tn(a, b):
    return lax.dot_general(a, b, (((1,), (1,)), ((0,), (0,))), preferred_element_type=F32)


def _sigmoid(x):
    return 1.0 / (1.0 + jnp.exp(-x))


def _softplus(x):
    return jnp.maximum(x, 0.0) + jnp.log(1.0 + jnp.exp(-jnp.abs(x)))


def _params(sem):
    return pltpu.CompilerParams(dimension_semantics=sem, vmem_limit_bytes=VMEM_LIMIT)


def _mod_kernel(c_ref, w_ref, b_ref, o_ref):
    c = c_ref[...]
    sc = (c * _sigmoid(c)).astype(BF16)
    o_ref[...] = _dot(sc, w_ref[0].astype(BF16)) + b_ref[...]


def _modulation(c, w_ada_all, layer, b_ada, tn=1536):
    bsz, d = c.shape
    n = w_ada_all.shape[2]
    return pl.pallas_call(
        _mod_kernel,
        grid=(n // tn,),
        in_specs=[
            pl.BlockSpec((bsz, d), lambda j: (0, 0)),
            pl.BlockSpec((1, d, tn), lambda j: (layer, 0, j)),
            pl.BlockSpec((1, tn), lambda j: (0, j)),
        ],
        out_specs=pl.BlockSpec((bsz, tn), lambda j: (0, j)),
        out_shape=jax.ShapeDtypeStruct((bsz, n), F32),
        compiler_params=_params(("arbitrary",)),
        name="adaln_mod",
    )(c, w_ada_all, b_ada.reshape(1, n))


def _inproj_kernel(x_ref, mod_ref, g_ref, w_ref, bg_ref, bd_ref, qg_ref, kg_ref,
                   rw_ref, gate_ref, q_ref, k_ref, v_ref):
    x = x_ref[0]
    ms = jnp.mean(x * x, axis=-1, keepdims=True)
    h = x * lax.rsqrt(ms + RMS_EPS) * g_ref[...]
    h = h * (1.0 + mod_ref[0, 1:2, :]) + mod_ref[0, 0:1, :]
    hb = h.astype(BF16)
    cw = MIX_WIDTH
    for j in range(RWKV_PAD // cw):
        rw_ref[0, :, j * cw:(j + 1) * cw] = _dot(hb, w_ref[:, j * cw:(j + 1) * cw])
    off = RWKV_PAD
    for j in range(2 * x.shape[1] // cw):
        pre = _dot(hb, w_ref[:, off + j * cw:off + (j + 1) * cw]) + bg_ref[:, j * cw:(j + 1) * cw]
        gate_ref[0, :, j * cw:(j + 1) * cw] = _sigmoid(pre).astype(BF16)
    off += 2 * x.shape[1]

    def head_norm(t, gain):
        msq = _dot((t * t).astype(BF16), bd_ref[...]) * (1.0 / HEAD_DIM)
        return t * lax.rsqrt(msq + RMS_EPS) * gain

    q = head_norm(_dot(hb, w_ref[:, off:off + cw]), qg_ref[...]) * (HEAD_DIM ** -0.5)
    k = head_norm(_dot(hb, w_ref[:, off + cw:off + 2 * cw]), kg_ref[...])
    q_ref[0] = q.astype(BF16)
    k_ref[0] = k.astype(BF16)
    v_ref[0] = _dot(hb, w_ref[:, off + 2 * cw:off + 3 * cw]).astype(BF16)


def _in_projection(x, mod, norm1_g, w_all, b_gate, bd_ones, qg, kg, tm=512):
    bsz, s, d = x.shape
    n_all = w_all.shape[1]
    head_shape = jax.ShapeDtypeStruct((bsz, s, MIX_WIDTH), BF16)
    head_spec = pl.BlockSpec((1, tm, MIX_WIDTH), lambda b, i: (b, i, 0))
    return pl.pallas_call(
        _inproj_kernel,
        grid=(bsz, s // tm),
        in_specs=[
            pl.BlockSpec((1, tm, d), lambda b, i: (b, i, 0)),
            pl.BlockSpec((1, 6, d), lambda b, i: (b, 0, 0)),
            pl.BlockSpec((1, d), lambda b, i: (0, 0)),
            pl.BlockSpec((d, n_all), lambda b, i: (0, 0), pipeline_mode=pl.Buffered(1)),
            pl.BlockSpec((1, 2 * d), lambda b, i: (0, 0)),
            pl.BlockSpec((MIX_WIDTH, MIX_WIDTH), lambda b, i: (0, 0)),
            pl.BlockSpec((1, MIX_WIDTH), lambda b, i: (0, 0)),
            pl.BlockSpec((1, MIX_WIDTH), lambda b, i: (0, 0)),
        ],
        out_specs=[
            pl.BlockSpec((1, tm, RWKV_PAD), lambda b, i: (b, i, 0)),
            pl.BlockSpec((1, tm, 2 * d), lambda b, i: (b, i, 0)),
            head_spec, head_spec, head_spec,
        ],
        out_shape=[
            jax.ShapeDtypeStruct((bsz, s, RWKV_PAD), F32),
            jax.ShapeDtypeStruct((bsz, s, 2 * d), BF16),
            head_shape, head_shape, head_shape,
        ],
        compiler_params=_params(("arbitrary", "arbitrary")),
        name="in_proj",
    )(x, mod, norm1_g, w_all, b_gate, bd_ones, qg, kg)


def _rwkv_kernel(f_ref, mu_ref, w0_ref, a0_ref, kkw_ref, kaw_ref, rk_ref, lng_ref, lnb_ref,
                 wl_ref, wg_ref, bd_ref, cum_ref, tot_ref,
                 y_ref,
                 prev_s, h_s, at_s, rt_s, bt_s, kt_s, bh_s, kh_s, v_s, gc_s, yn_s):
    ts = f_ref.shape[1]
    n_chunks = ts // CHUNK
    w = MIX_WIDTH

    @pl.when(pl.program_id(1) == 0)
    def _():
        prev_s[...] = jnp.zeros_like(prev_s)
        h_s[...] = jnp.zeros_like(h_s)

    f = f_ref[0]
    row = lax.broadcasted_iota(jnp.int32, f.shape, 0)
    shifted = jnp.where(row == 0, prev_s[...], pltpu.roll(f, 1, axis=0))
    prev_s[...] = f[ts - 1:ts, :]
    f = f + (shifted - f) * mu_ref[...]

    r = f[:, 0:w]
    k = f[:, w:2 * w]
    v = f[:, 2 * w:3 * w]
    lora_in = f[:, 3 * w:3 * w + 128]
    gate_in = f[:, 3 * w + 128:3 * w + 384]
    lane = lax.broadcasted_iota(jnp.int32, lora_in.shape, 1)
    lora_act = jnp.where(lane < DECAY_LORA, jnp.tanh(lora_in), lora_in)
    lora = _dot(lora_act.astype(BF16), wl_ref[...])
    g = _dot(_sigmoid(gate_in).astype(BF16), wg_ref[...])

    w_log = -_softplus(-(w0_ref[...] + lora[:, 0:w])) - 0.5
    ld = -jnp.exp(w_log)
    a = _sigmoid(a0_ref[...] + lora[:, w:2 * w])
    kk = k * kkw_ref[...]
    ss = _dot((kk * kk).astype(BF16), bd_ref[...])
    kkn = kk * lax.rsqrt(jnp.maximum(ss, L2_EPS * L2_EPS))
    k2 = k * (1.0 + (a - 1.0) * kaw_ref[...])
    b = kkn * a
    bonus = _dot((r * k2 * rk_ref[...]).astype(BF16), bd_ref[...]) * v

    ld_hi = ld.astype(BF16)
    ld_lo = (ld - ld_hi.astype(F32)).astype(BF16)
    cl = _dot(cum_ref[...], ld_hi) + _dot(cum_ref[...], ld_lo)
    cle = _dot(tot_ref[...], ld_hi) + _dot(tot_ref[...], ld_lo)
    e_neg = jnp.exp(-cl)
    e_end = jnp.exp(cle - cl)
    at_s[...] = (-kkn * jnp.exp(cl - ld)).astype(BF16)
    rt_s[...] = (r * jnp.exp(cl)).astype(BF16)
    bt_s[...] = (b * e_neg).astype(BF16)
    kt_s[...] = (k2 * e_neg).astype(BF16)
    bh_s[...] = (b * e_end).astype(BF16)
    kh_s[...] = (k2 * e_end).astype(BF16)
    v_s[...] = v.astype(BF16)
    gc_s[...] = jnp.exp(cle)

    ri = lax.broadcasted_iota(jnp.int32, (CHUNK, CHUNK), 0)
    ci = lax.broadcasted_iota(jnp.int32, (CHUNK, CHUNK), 1)
    strict = ri > ci
    incl = ri >= ci
    eye = ri == ci

    def blocks(ref):
        return jnp.stack([ref[c * CHUNK:(c + 1) * CHUNK, hd * HEAD_DIM:(hd + 1) * HEAD_DIM]
                          for c in range(n_chunks) for hd in range(N_HEADS)])

    at, rt, bt, kt, bh, kh, vv = (blocks(s) for s in (at_s, rt_s, bt_s, kt_s, bh_s, kh_s, v_s))
    a_ab = jnp.where(strict, _bmm_nt(at, bt), 0.0)
    a_ak = jnp.where(strict, _bmm_nt(at, kt), 0.0).astype(BF16)
    a_rb = jnp.where(incl, _bmm_nt(rt, bt), 0.0).astype(BF16)
    a_rk = jnp.where(incl, _bmm_nt(rt, kt), 0.0).astype(BF16)
    t = jnp.where(eye, 1.0, a_ab)
    p = a_ab
    for _ in range(5):
        pb = p.astype(BF16)
        p = _bmm(pb, pb)
        t = t + _bmm(t.astype(BF16), p.astype(BF16))
    tb = t.astype(BF16)
    p1 = _bmm(tb, at).astype(BF16)
    p2 = _bmm(tb, _bmm(a_ak, vv).astype(BF16)).astype(BF16)
    q1 = (rt.astype(F32) + _bmm(a_rb, p1)).astype(BF16)
    q2 = _bmm(a_rb, p2) + _bmm(a_rk, vv)
    gc = jnp.stack([gc_s[c * CHUNK:c * CHUNK + 1, hd * HEAD_DIM:(hd + 1) * HEAD_DIM]
                    for c in range(n_chunks) for hd in range(N_HEADS)])
    m = (jnp.where(eye, gc, 0.0) + _bmm_tn(bh, p1)).astype(BF16)
    n = _bmm_tn(bh, p2) + _bmm_tn(kh, vv)

    h = h_s[...]
    for c in range(n_chunks):
        sl = slice(c * N_HEADS, (c + 1) * N_HEADS)
        hb = h.astype(BF16)
        y = _bmm(q1[sl], hb) + q2[sl]
        h = _bmm(m[sl], hb) + n[sl]
        mean = jnp.mean(y, axis=-1, keepdims=True)
        yc = y - mean
        var = jnp.mean(yc * yc, axis=-1, keepdims=True)
        yn = yc * lax.rsqrt(var + GN_EPS)
        for hd in range(N_HEADS):
            yn_s[c * CHUNK:(c + 1) * CHUNK, hd * HEAD_DIM:(hd + 1) * HEAD_DIM] = yn[hd]
    h_s[...] = h

    y = (yn_s[...] * lng_ref[...] + lnb_ref[...] + bonus) * g
    y_ref[0] = y.astype(BF16)


def _sb_kernel(qi, q_ref, k_ref, v_ref, suffix_ref, o_ref, z_s, sp_s, acc_s, carry_s):
    tq = q_ref.shape[1]
    tk = suffix_ref.shape[0]
    ratio = tq // tk
    suffix = suffix_ref[...]
    lane = lax.broadcasted_iota(jnp.int32, (tq, 2 * HEAD_DIM), 1)
    qp = q_ref[0]
    zero = jnp.zeros_like(qp)
    q_heads = (jnp.where(lane < HEAD_DIM, qp, zero), jnp.where(lane >= HEAD_DIM, qp, zero))

    def key_block(n):
        j = jnp.maximum(qi * ratio + (ratio - 1) - n, 0)
        return j, pl.ds(pl.multiple_of(j * tk, tk), tk)

    def first_row(n):
        return (ratio - 1 - n) * tk if is_own(n) else 0

    def is_own(n):
        return isinstance(n, int) and n < ratio

    def scores(n, slot):
        j, rows = key_block(n)
        r0 = first_row(n)
        kb = k_ref[0, rows, :]
        for hd in range(2):
            z = _dot_nt(q_heads[hd][r0:, :], kb)
            zb = z.astype(BF16)
            sp = jnp.maximum(zb, 0.0) + jnp.log(1.0 + jnp.exp2(jnp.abs(zb) * (-LOG2E)))
            if is_own(n):
                ri = lax.broadcasted_iota(jnp.int32, z.shape, 0)
                ci = lax.broadcasted_iota(jnp.int32, z.shape, 1)
                causal = ci < ri
                sp = jnp.where(causal, sp, jnp.zeros_like(sp))
                z = jnp.where(causal, z, MASKED_SCORE)
            z_s[hd, slot, r0:, :] = z
            sp_s[hd, slot, r0:, :] = sp

    def accumulate(n, slot):
        _, rows = key_block(n)
        r0 = first_row(n)
        vb = v_ref[0, rows, :]
        for hd in range(2):
            cs = _dot(sp_s[hd, slot, r0:, :], suffix)
            carry = carry_s[hd, r0:, :]
            a = jnp.exp(z_s[hd, slot, r0:, :] - cs - jnp.concatenate([carry] * (tk // 128), axis=1))
            acc_s[hd, r0:, :] += _dot(a.astype(BF16), vb)
            carry_s[hd, r0:, :] = carry + jnp.broadcast_to(cs[:, 0:1], carry.shape)

    acc_s[...] = jnp.zeros_like(acc_s)
    carry_s[...] = jnp.zeros_like(carry_s)
    scores(0, 0)
    for n in range(ratio):
        scores(n + 1, (n + 1) % 2)
        accumulate(n, n % 2)

    def group(p, _):
        n0 = ratio + SB_UNROLL * p
        for u in range(SB_UNROLL):
            scores(n0 + u + 1, (u + 1) % 2)
            accumulate(n0 + u, u % 2)
        return 0

    lax.fori_loop(0, qi * (ratio // SB_UNROLL), group, 0)
    o_ref[0] = jnp.where(lane < HEAD_DIM, acc_s[0], acc_s[1]).astype(BF16)


N_RWKV_IN = 14
N_RWKV_SCRATCH = 11


def _mix_kernel(*refs):
    rw_in, sb_in = refs[:N_RWKV_IN], refs[N_RWKV_IN:N_RWKV_IN + 4]
    y_ref, o_ref = refs[N_RWKV_IN + 4:N_RWKV_IN + 6]
    scratch = refs[N_RWKV_IN + 6:]
    n_q = sb_in[1].shape[1] // sb_in[0].shape[1]
    _rwkv_kernel(*rw_in, y_ref, *scratch[:N_RWKV_SCRATCH])
    _sb_kernel(pl.program_id(1) % n_q, *sb_in, o_ref, *scratch[N_RWKV_SCRATCH:])


def _token_mixers(f_rwkv, mu, w0, a0, k_k, k_a, r_k, lnx_g, lnx_b, wl, wg, bd_ones, q, k, v,
                  ts=256, tq=1024, tk=256):
    bsz, s, fw = f_rwkv.shape
    w = MIX_WIDTH
    pw = 2 * HEAD_DIM
    n_q = s // tq
    assert s // ts == (w // pw) * n_q
    assert (tq // tk) % SB_UNROLL == 0 and SB_UNROLL % 2 == 0
    ri = jnp.arange(ts)[:, None]
    ci = jnp.arange(ts)[None, :]
    same = (ri // CHUNK) == (ci // CHUNK)
    cum = (same & (ci <= ri)).astype(BF16)
    tot = same.astype(BF16)
    suffix = (jnp.arange(tk)[:, None] >= jnp.arange(tk)[None, :]).astype(BF16)
    vec = lambda n: pl.BlockSpec((1, n), lambda b, i: (0, 0))
    full = lambda a: pl.BlockSpec(a.shape, lambda b, i: (0, 0))
    return pl.pallas_call(
        _mix_kernel,
        grid=(bsz, s // ts),
        in_specs=[
            pl.BlockSpec((1, ts, fw), lambda b, i: (b, i, 0)),
            vec(fw), vec(w), vec(w), vec(w), vec(w), vec(w), vec(w), vec(w),
            full(wl), full(wg), full(bd_ones), full(cum), full(tot),
            pl.BlockSpec((1, tq, pw), lambda b, i: (b, i % n_q, i // n_q)),
            pl.BlockSpec((1, s, pw), lambda b, i: (b, 0, i // n_q)),
            pl.BlockSpec((1, s, pw), lambda b, i: (b, 0, i // n_q)),
            pl.BlockSpec((tk, tk), lambda b, i: (0, 0)),
        ],
        out_specs=[
            pl.BlockSpec((1, ts, w), lambda b, i: (b, i, 0)),
            pl.BlockSpec((1, tq, pw), lambda b, i: (b, i % n_q, i // n_q)),
        ],
        out_shape=[jax.ShapeDtypeStruct((bsz, s, w), BF16), jax.ShapeDtypeStruct((bsz, s, w), BF16)],
        scratch_shapes=[
            pltpu.VMEM((1, fw), F32),
            pltpu.VMEM((N_HEADS, HEAD_DIM, HEAD_DIM), F32),
            pltpu.VMEM((ts, w), BF16), pltpu.VMEM((ts, w), BF16), pltpu.VMEM((ts, w), BF16),
            pltpu.VMEM((ts, w), BF16), pltpu.VMEM((ts, w), BF16), pltpu.VMEM((ts, w), BF16),
            pltpu.VMEM((ts, w), BF16),
            pltpu.VMEM((ts, w), F32),
            pltpu.VMEM((ts, w), F32),
            pltpu.VMEM((2, 2, tq, tk), F32),
            pltpu.VMEM((2, 2, tq, tk), BF16),
            pltpu.VMEM((2, tq, pw), F32),
            pltpu.VMEM((2, tq, pw), F32),
        ],
        compiler_params=_params(("arbitrary", "arbitrary")),
        name="token_mixers",
    )(f_rwkv, mu, w0, a0, k_k, k_a, r_k, lnx_g, lnx_b, wl, wg, bd_ones, cum, tot, q, k, v, suffix)


def _merge_kernel(x_ref, yr_ref, ys_ref, gate_ref, mod_ref, wor_ref, wos_ref, wout_ref, o_ref):
    d = x_ref.shape[2]
    gr = gate_ref[0, :, 0:d].astype(F32)
    gs = gate_ref[0, :, d:2 * d].astype(F32)
    merged = gr * _dot(yr_ref[0], wor_ref[...]) + gs * _dot(ys_ref[0], wos_ref[...])
    o_ref[0] = x_ref[0] + mod_ref[0, 2:3, :] * _dot(merged.astype(BF16), wout_ref[...])


def _merge(x, y_rwkv, y_sb, gates, mod, w_or, w_os, w_out, tm=512):
    bsz, s, d = x.shape
    full = lambda a: pl.BlockSpec(a.shape, lambda b, i: (0, 0))
    return pl.pallas_call(
        _merge_kernel,
        grid=(bsz, s // tm),
        in_specs=[
            pl.BlockSpec((1, tm, d), lambda b, i: (b, i, 0)),
            pl.BlockSpec((1, tm, MIX_WIDTH), lambda b, i: (b, i, 0)),
            pl.BlockSpec((1, tm, MIX_WIDTH), lambda b, i: (b, i, 0)),
            pl.BlockSpec((1, tm, 2 * d), lambda b, i: (b, i, 0)),
            pl.BlockSpec((1, 6, d), lambda b, i: (b, 0, 0)),
            full(w_or), full(w_os), full(w_out),
        ],
        out_specs=pl.BlockSpec((1, tm, d), lambda b, i: (b, i, 0)),
        out_shape=jax.ShapeDtypeStruct((bsz, s, d), F32),
        compiler_params=_params(("arbitrary", "arbitrary")),
        name="merge_out",
    )(x, y_rwkv, y_sb, gates, mod, w_or, w_os, w_out)


def _ffn_kernel(x_ref, halo_ref, mod_ref, g_ref, wup_ref, cw_ref, cb_ref, wdn_ref, o_ref, *u_bufs, tf):
    tm = x_ref.shape[1]
    dff = wdn_ref.shape[0]
    x = x_ref[0]

    def norm_mod(t):
        ms = jnp.mean(t * t, axis=-1, keepdims=True)
        hh = t * lax.rsqrt(ms + RMS_EPS) * g_ref[...]
        return hh * (1.0 + mod_ref[0, 4:5, :]) + mod_ref[0, 3:4, :]

    h = norm_mod(x).astype(BF16)
    keep = (pl.program_id(1) > 0).astype(F32)
    hh = (norm_mod(halo_ref[0]) * keep).astype(BF16)
    hcat = jnp.concatenate([hh, h], axis=0)

    def conv(col0):
        cols = slice(col0, col0 + tf)
        u_s = u_bufs[(col0 // tf) % len(u_bufs)]
        u_s[...] = _dot(hcat, wup_ref[:, cols])
        cw = cw_ref[:, cols]
        out = cb_ref[:, cols] + cw[2:3, :] * u_s[CONV_HALO:, :]
        out = out + cw[1:2, :] * u_s[CONV_HALO - 1:CONV_HALO - 1 + tm, :]
        out = out + cw[0:1, :] * u_s[CONV_HALO - 2:CONV_HALO - 2 + tm, :]
        return out

    def act(j):
        val = conv(j * tf)
        gt = conv(dff + j * tf)
        return (gt * _sigmoid(gt) * val).astype(BF16)

    n_tiles = dff // tf
    acc = jnp.zeros((tm, x.shape[1]), F32)
    nxt = act(0)
    for j in range(n_tiles):
        cur = nxt
        if j + 1 < n_tiles:
            nxt = act(j + 1)
        acc = acc + _dot(cur, wdn_ref[j * tf:(j + 1) * tf, :])
    o_ref[0] = x + mod_ref[0, 5:6, :] * acc


def _ffn(x, mod, norm2_g, w_up, conv_w, conv_b, w_down, tm=512, tf=MXU_TILE):
    bsz, s, d = x.shape
    dff = w_down.shape[0]
    pad = -dff % tf
    halves = lambda a: jnp.concatenate(
        [jnp.pad(a[..., :dff], ((0, 0), (0, pad))), jnp.pad(a[..., dff:], ((0, 0), (0, pad)))], axis=-1)
    w_up = halves(w_up).astype(BF16)
    conv_w = halves(conv_w)
    conv_b = halves(conv_b)
    w_down = jnp.pad(w_down, ((0, pad), (0, 0))).astype(BF16)
    hb = tm // CONV_HALO
    full = lambda a: pl.BlockSpec(a.shape, lambda b, i: (0, 0), pipeline_mode=pl.Buffered(1))
    return pl.pallas_call(
        functools.partial(_ffn_kernel, tf=tf),
        grid=(bsz, s // tm),
        in_specs=[
            pl.BlockSpec((1, tm, d), lambda b, i: (b, i, 0)),
            pl.BlockSpec((1, CONV_HALO, d), lambda b, i: (b, jnp.maximum(i * hb - 1, 0), 0)),
            pl.BlockSpec((1, 6, d), lambda b, i: (b, 0, 0)),
            pl.BlockSpec((1, d), lambda b, i: (0, 0)),
            full(w_up), full(conv_w), full(conv_b), full(w_down),
        ],
        out_specs=pl.BlockSpec((1, tm, d), lambda b, i: (b, i, 0)),
        out_shape=jax.ShapeDtypeStruct((bsz, s, d), F32),
        scratch_shapes=[pltpu.VMEM((CONV_HALO + tm, tf), F32) for _ in range(4)],
        compiler_params=_params(("arbitrary", "arbitrary")),
        name="conv_ffn",
    )(x, x, mod, norm2_g, w_up, conv_w, conv_b, w_down)


def _layer(x, c, w_ada_all, layer, b_ada, norm1_g, w_in, b_gate, rwkv_mu, rwkv_w0, rwkv_w2, rwkv_a0,
           rwkv_a2, rwkv_g2, rwkv_k_k, rwkv_k_a, rwkv_r_k, rwkv_lnx_g, rwkv_lnx_b, sb_q_g,
           sb_k_g, w_o_rwkv, w_o_sb, w_out, norm2_g, w_up, conv_w, conv_b, w_down):
    bsz, s, d = x.shape
    w = MIX_WIDTH
    rwkv_in = 3 * w + DECAY_LORA + ICLR_LORA + GATE_LORA
    sb_in = 3 * w
    pad = RWKV_PAD - rwkv_in

    w_all = jnp.concatenate([
        w_in[:, :rwkv_in], jnp.zeros((d, pad), F32),
        w_in[:, rwkv_in + sb_in:], w_in[:, rwkv_in:rwkv_in + sb_in]], axis=1).astype(BF16)
    mu = jnp.concatenate([rwkv_mu, jnp.zeros((pad,), F32)]).reshape(1, RWKV_PAD)
    wl = jnp.zeros((DECAY_LORA + ICLR_LORA, 2 * w), F32)
    wl = wl.at[:DECAY_LORA, :w].set(rwkv_w2).at[DECAY_LORA:, w:].set(rwkv_a2).astype(BF16)
    wg = jnp.zeros((256, w), F32).at[:GATE_LORA].set(rwkv_g2).astype(BF16)
    head_id = jnp.arange(w) // HEAD_DIM
    bd_ones = (head_id[:, None] == head_id[None, :]).astype(BF16)
    row = lambda a: a.reshape(1, -1)

    mod = _modulation(c, w_ada_all, layer, b_ada).reshape(bsz, 6, d)
    f_rwkv, gates, q, k, v = _in_projection(
        x, mod, row(norm1_g), w_all, row(b_gate), bd_ones,
        row(jnp.tile(sb_q_g, N_HEADS)), row(jnp.tile(sb_k_g, N_HEADS)))
    y_rwkv, y_sb = _token_mixers(
        f_rwkv, mu, row(rwkv_w0), row(rwkv_a0), row(rwkv_k_k), row(rwkv_k_a), row(rwkv_r_k),
        row(rwkv_lnx_g), row(rwkv_lnx_b), wl, wg, bd_ones, q, k, v)
    x = _merge(x, y_rwkv, y_sb, gates, mod,
               w_o_rwkv.astype(BF16), w_o_sb.astype(BF16), w_out.astype(BF16))
    return _ffn(x, mod, row(norm2_g), w_up, conv_w, row(conv_b), w_down)


def kernel(x, c, w_ada, b_ada, norm1_g, w_in, b_gate, rwkv_mu, rwkv_w0, rwkv_w2, rwkv_a0, rwkv_a2, rwkv_g2, rwkv_k_k, rwkv_k_a, rwkv_r_k, rwkv_lnx_g, rwkv_lnx_b, sb_q_g, sb_k_g, w_o_rwkv, w_o_sb, w_out, norm2_g, w_up, conv_w, conv_b, w_down):
    params = (b_ada, norm1_g, w_in, b_gate, rwkv_mu, rwkv_w0, rwkv_w2, rwkv_a0, rwkv_a2,
              rwkv_g2, rwkv_k_k, rwkv_k_a, rwkv_r_k, rwkv_lnx_g, rwkv_lnx_b, sb_q_g, sb_k_g,
              w_o_rwkv, w_o_sb, w_out, norm2_g, w_up, conv_w, conv_b, w_down)
    for layer in range(w_ada.shape[0]):
        x = _layer(x, c, w_ada, layer, *(p[layer] for p in params))
    return x
```

```python
import functools

import jax
import jax.numpy as jnp
from jax import lax
from jax.experimental import pallas as pl
from jax.experimental.pallas import tpu as pltpu

F32 = jnp.float32
BF16 = jnp.bfloat16

HEAD_DIM = 64
N_HEADS = 8
MIX_WIDTH = HEAD_DIM * N_HEADS
DECAY_LORA = 64
ICLR_LORA = 64
GATE_LORA = 160
RWKV_PAD = 2048
CONV_WIDTH = 3
CONV_HALO = 8
RMS_EPS = 1e-6
GN_EPS = 64e-5
L2_EPS = 1e-12
CHUNK = 64
LOG2E = 1.4426950408889634
MASKED_SCORE = -1e30
SB_UNROLL = 4
MXU_TILE = 256

VMEM_LIMIT = 56 * 1024 * 1024


def _dot(a, b):
    return jnp.dot(a, b, preferred_element_type=F32)


def _dot_nt(a, b):
    return lax.dot_general(a, b, (((1,), (1,)), ((), ())), preferred_element_type=F32)


def _bmm(a, b):
    return lax.dot_general(a, b, (((2,), (1,)), ((0,), (0,))), preferred_element_type=F32)


def _bmm_nt(a, b):
    return lax.dot_general(a, b, (((2,), (2,)), ((0,), (0,))), preferred_element_type=F32)


def _bmm_tn(a, b):
    return lax.dot_general(a, b, (((1,), (1,)), ((0,), (0,))), preferred_element_type=F32)


def _sigmoid(x):
    return 1.0 / (1.0 + jnp.exp(-x))


def _softplus(x):
    return jnp.maximum(x, 0.0) + jnp.log(1.0 + jnp.exp(-jnp.abs(x)))


def _params(sem):
    return pltpu.CompilerParams(dimension_semantics=sem, vmem_limit_bytes=VMEM_LIMIT)


def _mod_kernel(c_ref, w_ref, b_ref, o_ref):
    c = c_ref[...]
    sc = (c * _sigmoid(c)).astype(BF16)
    o_ref[...] = _dot(sc, w_ref[...].astype(BF16)) + b_ref[...]


def _modulation(c, w_ada, b_ada, tn=1536):
    bsz, d = c.shape
    n = w_ada.shape[1]
    return pl.pallas_call(
        _mod_kernel,
        grid=(n // tn,),
        in_specs=[
            pl.BlockSpec((bsz, d), lambda j: (0, 0)),
            pl.BlockSpec((d, tn), lambda j: (0, j)),
            pl.BlockSpec((1, tn), lambda j: (0, j)),
        ],
        out_specs=pl.BlockSpec((bsz, tn), lambda j: (0, j)),
        out_shape=jax.ShapeDtypeStruct((bsz, n), F32),
        compiler_params=_params(("arbitrary",)),
        name="adaln_mod",
    )(c, w_ada, b_ada.reshape(1, n))


def _inproj_kernel(x_ref, mod_ref, g_ref, w_ref, bd_ref, qg_ref, kg_ref,
                   rw_ref, gate_ref, q_ref, k_ref, v_ref):
    x = x_ref[0]
    ms = jnp.mean(x * x, axis=-1, keepdims=True)
    h = x * lax.rsqrt(ms + RMS_EPS) * g_ref[...]
    h = h * (1.0 + mod_ref[0, 1:2, :]) + mod_ref[0, 0:1, :]
    hb = h.astype(BF16)
    cw = MIX_WIDTH
    for j in range(RWKV_PAD // cw):
        rw_ref[0, :, j * cw:(j + 1) * cw] = _dot(hb, w_ref[:, j * cw:(j + 1) * cw])
    off = RWKV_PAD
    for j in range(2 * x.shape[1] // cw):
        gate_ref[0, :, j * cw:(j + 1) * cw] = _dot(hb, w_ref[:, off + j * cw:off + (j + 1) * cw])
    off += 2 * x.shape[1]

    def head_norm(t, gain):
        msq = _dot((t * t).astype(BF16), bd_ref[...]) * (1.0 / HEAD_DIM)
        return t * lax.rsqrt(msq + RMS_EPS) * gain

    q = head_norm(_dot(hb, w_ref[:, off:off + cw]), qg_ref[...]) * (HEAD_DIM ** -0.5)
    k = head_norm(_dot(hb, w_ref[:, off + cw:off + 2 * cw]), kg_ref[...])
    q_ref[0] = q.astype(BF16)
    k_ref[0] = k.astype(BF16)
    v_ref[0] = _dot(hb, w_ref[:, off + 2 * cw:off + 3 * cw]).astype(BF16)


def _in_projection(x, mod, norm1_g, w_all, bd_ones, qg, kg, tm=512):
    bsz, s, d = x.shape
    n_all = w_all.shape[1]
    head_shape = jax.ShapeDtypeStruct((bsz, s, MIX_WIDTH), BF16)
    head_spec = pl.BlockSpec((1, tm, MIX_WIDTH), lambda b, i: (b, i, 0))
    return pl.pallas_call(
        _inproj_kernel,
        grid=(bsz, s // tm),
        in_specs=[
            pl.BlockSpec((1, tm, d), lambda b, i: (b, i, 0)),
            pl.BlockSpec((1, 6, d), lambda b, i: (b, 0, 0)),
            pl.BlockSpec((1, d), lambda b, i: (0, 0)),
            pl.BlockSpec((d, n_all), lambda b, i: (0, 0), pipeline_mode=pl.Buffered(1)),
            pl.BlockSpec((MIX_WIDTH, MIX_WIDTH), lambda b, i: (0, 0)),
            pl.BlockSpec((1, MIX_WIDTH), lambda b, i: (0, 0)),
            pl.BlockSpec((1, MIX_WIDTH), lambda b, i: (0, 0)),
        ],
        out_specs=[
            pl.BlockSpec((1, tm, RWKV_PAD), lambda b, i: (b, i, 0)),
            pl.BlockSpec((1, tm, 2 * d), lambda b, i: (b, i, 0)),
            head_spec, head_spec, head_spec,
        ],
        out_shape=[
            jax.ShapeDtypeStruct((bsz, s, RWKV_PAD), F32),
            jax.ShapeDtypeStruct((bsz, s, 2 * d), F32),
            head_shape, head_shape, head_shape,
        ],
        compiler_params=_params(("arbitrary", "arbitrary")),
        name="in_proj",
    )(x, mod, norm1_g, w_all, bd_ones, qg, kg)


def _rwkv_kernel(f_ref, mu_ref, w0_ref, a0_ref, kkw_ref, kaw_ref, rk_ref, lng_ref, lnb_ref,
                 wl_ref, wg_ref, bd_ref, cum_ref, tot_ref,
                 y_ref,
                 prev_s, h_s, at_s, rt_s, bt_s, kt_s, bh_s, kh_s, v_s, gc_s, yn_s):
    ts = f_ref.shape[1]
    n_chunks = ts // CHUNK
    w = MIX_WIDTH

    @pl.when(pl.program_id(1) == 0)
    def _():
        prev_s[...] = jnp.zeros_like(prev_s)
        h_s[...] = jnp.zeros_like(h_s)

    f = f_ref[0]
    row = lax.broadcasted_iota(jnp.int32, f.shape, 0)
    shifted = jnp.where(row == 0, prev_s[...], pltpu.roll(f, 1, axis=0))
    prev_s[...] = f[ts - 1:ts, :]
    f = f + (shifted - f) * mu_ref[...]

    r = f[:, 0:w]
    k = f[:, w:2 * w]
    v = f[:, 2 * w:3 * w]
    lora_in = f[:, 3 * w:3 * w + 128]
    gate_in = f[:, 3 * w + 128:3 * w + 384]
    lane = lax.broadcasted_iota(jnp.int32, lora_in.shape, 1)
    lora_act = jnp.where(lane < DECAY_LORA, jnp.tanh(lora_in), lora_in)
    lora = _dot(lora_act.astype(BF16), wl_ref[...])
    g = _dot(_sigmoid(gate_in).astype(BF16), wg_ref[...])

    w_log = -_softplus(-(w0_ref[...] + lora[:, 0:w])) - 0.5
    ld = -jnp.exp(w_log)
    a = _sigmoid(a0_ref[...] + lora[:, w:2 * w])
    kk = k * kkw_ref[...]
    ss = _dot((kk * kk).astype(BF16), bd_ref[...])
    kkn = kk * lax.rsqrt(jnp.maximum(ss, L2_EPS * L2_EPS))
    k2 = k * (1.0 + (a - 1.0) * kaw_ref[...])
    b = kkn * a
    bonus = _dot((r * k2 * rk_ref[...]).astype(BF16), bd_ref[...]) * v

    ld_hi = ld.astype(BF16)
    ld_lo = (ld - ld_hi.astype(F32)).astype(BF16)
    cl = _dot(cum_ref[...], ld_hi) + _dot(cum_ref[...], ld_lo)
    cle = _dot(tot_ref[...], ld_hi) + _dot(tot_ref[...], ld_lo)
    e_neg = jnp.exp(-cl)
    e_end = jnp.exp(cle - cl)
    at_s[...] = (-kkn * jnp.exp(cl - ld)).astype(BF16)
    rt_s[...] = (r * jnp.exp(cl)).astype(BF16)
    bt_s[...] = (b * e_neg).astype(BF16)
    kt_s[...] = (k2 * e_neg).astype(BF16)
    bh_s[...] = (b * e_end).astype(BF16)
    kh_s[...] = (k2 * e_end).astype(BF16)
    v_s[...] = v.astype(BF16)
    gc_s[...] = jnp.exp(cle)

    ri = lax.broadcasted_iota(jnp.int32, (CHUNK, CHUNK), 0)
    ci = lax.broadcasted_iota(jnp.int32, (CHUNK, CHUNK), 1)
    strict = ri > ci
    incl = ri >= ci
    eye = ri == ci

    def blocks(ref):
        return jnp.stack([ref[c * CHUNK:(c + 1) * CHUNK, hd * HEAD_DIM:(hd + 1) * HEAD_DIM]
                          for c in range(n_chunks) for hd in range(N_HEADS)])

    at, rt, bt, kt, bh, kh, vv = (blocks(s) for s in (at_s, rt_s, bt_s, kt_s, bh_s, kh_s, v_s))
    a_ab = jnp.where(strict, _bmm_nt(at, bt), 0.0)
    a_ak = jnp.where(strict, _bmm_nt(at, kt), 0.0).astype(BF16)
    a_rb = jnp.where(incl, _bmm_nt(rt, bt), 0.0).astype(BF16)
    a_rk = jnp.where(incl, _bmm_nt(rt, kt), 0.0).astype(BF16)
    t = jnp.where(eye, 1.0, a_ab)
    p = a_ab
    for _ in range(5):
        pb = p.astype(BF16)
        p = _bmm(pb, pb)
        t = t + _bmm(t.astype(BF16), p.astype(BF16))
    tb = t.astype(BF16)
    p1 = _bmm(tb, at).astype(BF16)
    p2 = _bmm(tb, _bmm(a_ak, vv).astype(BF16)).astype(BF16)
    q1 = (rt.astype(F32) + _bmm(a_rb, p1)).astype(BF16)
    q2 = _bmm(a_rb, p2) + _bmm(a_rk, vv)
    gc = jnp.stack([gc_s[c * CHUNK:c * CHUNK + 1, hd * HEAD_DIM:(hd + 1) * HEAD_DIM]
                    for c in range(n_chunks) for hd in range(N_HEADS)])
    m = (jnp.where(eye, gc, 0.0) + _bmm_tn(bh, p1)).astype(BF16)
    n = _bmm_tn(bh, p2) + _bmm_tn(kh, vv)

    h = h_s[...]
    for c in range(n_chunks):
        sl = slice(c * N_HEADS, (c + 1) * N_HEADS)
        hb = h.astype(BF16)
        y = _bmm(q1[sl], hb) + q2[sl]
        h = _bmm(m[sl], hb) + n[sl]
        mean = jnp.mean(y, axis=-1, keepdims=True)
        yc = y - mean
        var = jnp.mean(yc * yc, axis=-1, keepdims=True)
        yn = yc * lax.rsqrt(var + GN_EPS)
        for hd in range(N_HEADS):
            yn_s[c * CHUNK:(c + 1) * CHUNK, hd * HEAD_DIM:(hd + 1) * HEAD_DIM] = yn[hd]
    h_s[...] = h

    y = (yn_s[...] * lng_ref[...] + lnb_ref[...] + bonus) * g
    y_ref[0] = y.astype(BF16)


def _sb_kernel(qi, q_ref, k_ref, v_ref, suffix_ref, o_ref, z_s, sp_s, acc_s, carry_s):
    tq = q_ref.shape[1]
    tk = suffix_ref.shape[0]
    ratio = tq // tk
    suffix = suffix_ref[...]
    lane = lax.broadcasted_iota(jnp.int32, (tq, 2 * HEAD_DIM), 1)
    qp = q_ref[0]
    zero = jnp.zeros_like(qp)
    q_heads = (jnp.where(lane < HEAD_DIM, qp, zero), jnp.where(lane >= HEAD_DIM, qp, zero))

    def key_block(n):
        j = jnp.maximum(qi * ratio + (ratio - 1) - n, 0)
        return j, pl.ds(pl.multiple_of(j * tk, tk), tk)

    def first_row(n):
        return (ratio - 1 - n) * tk if is_own(n) else 0

    def is_own(n):
        return isinstance(n, int) and n < ratio

    def scores(n, slot):
        j, rows = key_block(n)
        r0 = first_row(n)
        kb = k_ref[0, rows, :]
        for hd in range(2):
            z = _dot_nt(q_heads[hd][r0:, :], kb)
            zb = z.astype(BF16)
            sp = jnp.maximum(zb, 0.0) + jnp.log(1.0 + jnp.exp2(jnp.abs(zb) * (-LOG2E)))
            if is_own(n):
                ri = lax.broadcasted_iota(jnp.int32, z.shape, 0)
                ci = lax.broadcasted_iota(jnp.int32, z.shape, 1)
                causal = ci < ri
                sp = jnp.where(causal, sp, jnp.zeros_like(sp))
                z = jnp.where(causal, z, MASKED_SCORE)
            z_s[hd, slot, r0:, :] = z
            sp_s[hd, slot, r0:, :] = sp

    def accumulate(n, slot):
        _, rows = key_block(n)
        r0 = first_row(n)
        vb = v_ref[0, rows, :]
        for hd in range(2):
            cs = _dot(sp_s[hd, slot, r0:, :], suffix)
            carry = carry_s[hd, r0:, :]
            a = jnp.exp(z_s[hd, slot, r0:, :] - cs - jnp.concatenate([carry] * (tk // 128), axis=1))
            acc_s[hd, r0:, :] += _dot(a.astype(BF16), vb)
            carry_s[hd, r0:, :] = carry + jnp.broadcast_to(cs[:, 0:1], carry.shape)

    acc_s[...] = jnp.zeros_like(acc_s)
    carry_s[...] = jnp.zeros_like(carry_s)
    scores(0, 0)
    for n in range(ratio):
        scores(n + 1, (n + 1) % 2)
        accumulate(n, n % 2)

    def group(p, _):
        n0 = ratio + SB_UNROLL * p
        for u in range(SB_UNROLL):
            scores(n0 + u + 1, (u + 1) % 2)
            accumulate(n0 + u, u % 2)
        return 0

    lax.fori_loop(0, qi * (ratio // SB_UNROLL), group, 0)
    o_ref[0] = jnp.where(lane < HEAD_DIM, acc_s[0], acc_s[1]).astype(BF16)


N_RWKV_IN = 14
N_RWKV_SCRATCH = 11


def _mix_kernel(*refs):
    rw_in, sb_in = refs[:N_RWKV_IN], refs[N_RWKV_IN:N_RWKV_IN + 4]
    y_ref, o_ref = refs[N_RWKV_IN + 4:N_RWKV_IN + 6]
    scratch = refs[N_RWKV_IN + 6:]
    n_q = sb_in[1].shape[1] // sb_in[0].shape[1]
    _rwkv_kernel(*rw_in, y_ref, *scratch[:N_RWKV_SCRATCH])
    _sb_kernel(pl.program_id(1) % n_q, *sb_in, o_ref, *scratch[N_RWKV_SCRATCH:])


def _token_mixers(f_rwkv, mu, w0, a0, k_k, k_a, r_k, lnx_g, lnx_b, wl, wg, bd_ones, q, k, v,
                  ts=256, tq=1024, tk=256):
    bsz, s, fw = f_rwkv.shape
    w = MIX_WIDTH
    pw = 2 * HEAD_DIM
    n_q = s // tq
    assert s // ts == (w // pw) * n_q
    assert (tq // tk) % SB_UNROLL == 0 and SB_UNROLL % 2 == 0
    ri = jnp.arange(ts)[:, None]
    ci = jnp.arange(ts)[None, :]
    same = (ri // CHUNK) == (ci // CHUNK)
    cum = (same & (ci <= ri)).astype(BF16)
    tot = same.astype(BF16)
    suffix = (jnp.arange(tk)[:, None] >= jnp.arange(tk)[None, :]).astype(BF16)
    vec = lambda n: pl.BlockSpec((1, n), lambda b, i: (0, 0))
    full = lambda a: pl.BlockSpec(a.shape, lambda b, i: (0, 0))
    return pl.pallas_call(
        _mix_kernel,
        grid=(bsz, s // ts),
        in_specs=[
            pl.BlockSpec((1, ts, fw), lambda b, i: (b, i, 0)),
            vec(fw), vec(w), vec(w), vec(w), vec(w), vec(w), vec(w), vec(w),
            full(wl), full(wg), full(bd_ones), full(cum), full(tot),
            pl.BlockSpec((1, tq, pw), lambda b, i: (b, i % n_q, i // n_q)),
            pl.BlockSpec((1, s, pw), lambda b, i: (b, 0, i // n_q)),
            pl.BlockSpec((1, s, pw), lambda b, i: (b, 0, i // n_q)),
            pl.BlockSpec((tk, tk), lambda b, i: (0, 0)),
        ],
        out_specs=[
            pl.BlockSpec((1, ts, w), lambda b, i: (b, i, 0)),
            pl.BlockSpec((1, tq, pw), lambda b, i: (b, i % n_q, i // n_q)),
        ],
        out_shape=[jax.ShapeDtypeStruct((bsz, s, w), BF16), jax.ShapeDtypeStruct((bsz, s, w), BF16)],
        scratch_shapes=[
            pltpu.VMEM((1, fw), F32),
            pltpu.VMEM((N_HEADS, HEAD_DIM, HEAD_DIM), F32),
            pltpu.VMEM((ts, w), BF16), pltpu.VMEM((ts, w), BF16), pltpu.VMEM((ts, w), BF16),
            pltpu.VMEM((ts, w), BF16), pltpu.VMEM((ts, w), BF16), pltpu.VMEM((ts, w), BF16),
            pltpu.VMEM((ts, w), BF16),
            pltpu.VMEM((ts, w), F32),
            pltpu.VMEM((ts, w), F32),
            pltpu.VMEM((2, 2, tq, tk), F32),
            pltpu.VMEM((2, 2, tq, tk), BF16),
            pltpu.VMEM((2, tq, pw), F32),
            pltpu.VMEM((2, tq, pw), F32),
        ],
        compiler_params=_params(("arbitrary", "arbitrary")),
        name="token_mixers",
    )(f_rwkv, mu, w0, a0, k_k, k_a, r_k, lnx_g, lnx_b, wl, wg, bd_ones, cum, tot, q, k, v, suffix)


def _merge_kernel(x_ref, yr_ref, ys_ref, gate_ref, bg_ref, mod_ref, wor_ref, wos_ref, wout_ref, o_ref):
    d = x_ref.shape[2]
    gr = _sigmoid(gate_ref[0, :, 0:d] + bg_ref[:, 0:d])
    gs = _sigmoid(gate_ref[0, :, d:2 * d] + bg_ref[:, d:2 * d])
    merged = gr * _dot(yr_ref[0], wor_ref[...]) + gs * _dot(ys_ref[0], wos_ref[...])
    o_ref[0] = x_ref[0] + mod_ref[0, 2:3, :] * _dot(merged.astype(BF16), wout_ref[...])


def _merge(x, y_rwkv, y_sb, gates, b_gate, mod, w_or, w_os, w_out, tm=512):
    bsz, s, d = x.shape
    full = lambda a: pl.BlockSpec(a.shape, lambda b, i: (0, 0))
    return pl.pallas_call(
        _merge_kernel,
        grid=(bsz, s // tm),
        in_specs=[
            pl.BlockSpec((1, tm, d), lambda b, i: (b, i, 0)),
            pl.BlockSpec((1, tm, MIX_WIDTH), lambda b, i: (b, i, 0)),
            pl.BlockSpec((1, tm, MIX_WIDTH), lambda b, i: (b, i, 0)),
            pl.BlockSpec((1, tm, 2 * d), lambda b, i: (b, i, 0)),
            pl.BlockSpec((1, 2 * d), lambda b, i: (0, 0)),
            pl.BlockSpec((1, 6, d), lambda b, i: (b, 0, 0)),
            full(w_or), full(w_os), full(w_out),
        ],
        out_specs=pl.BlockSpec((1, tm, d), lambda b, i: (b, i, 0)),
        out_shape=jax.ShapeDtypeStruct((bsz, s, d), F32),
        compiler_params=_params(("arbitrary", "arbitrary")),
        name="merge_out",
    )(x, y_rwkv, y_sb, gates, b_gate, mod, w_or, w_os, w_out)


def _ffn_kernel(x_ref, halo_ref, mod_ref, g_ref, wup_ref, cw_ref, cb_ref, wdn_ref, o_ref, act_s, *u_bufs, tf):
    tm = x_ref.shape[1]
    dff = wdn_ref.shape[0]
    x = x_ref[0]

    def norm_mod(t):
        ms = jnp.mean(t * t, axis=-1, keepdims=True)
        hh = t * lax.rsqrt(ms + RMS_EPS) * g_ref[...]
        return hh * (1.0 + mod_ref[0, 4:5, :]) + mod_ref[0, 3:4, :]

    h = norm_mod(x).astype(BF16)
    keep = (pl.program_id(1) > 0).astype(F32)
    hh = (norm_mod(halo_ref[0]) * keep).astype(BF16)
    hcat = jnp.concatenate([hh, h], axis=0)

    n_tiles = dff // tf

    def up(j):
        for half in range(2):
            col0 = half * dff + j * tf
            u_bufs[(2 * j + half) % len(u_bufs)][...] = _dot(hcat, wup_ref[:, col0:col0 + tf])

    def conv(j, half):
        cols = slice(half * dff + j * tf, half * dff + (j + 1) * tf)
        u_s = u_bufs[(2 * j + half) % len(u_bufs)]
        cw = cw_ref[:, cols]
        out = cb_ref[:, cols] + cw[2:3, :] * u_s[CONV_HALO:, :]
        out = out + cw[1:2, :] * u_s[CONV_HALO - 1:CONV_HALO - 1 + tm, :]
        out = out + cw[0:1, :] * u_s[CONV_HALO - 2:CONV_HALO - 2 + tm, :]
        return out

    def act(j):
        gt = conv(j, 1)
        act_s[:, j * tf:(j + 1) * tf] = (gt * _sigmoid(gt) * conv(j, 0)).astype(BF16)

    up(0)
    if n_tiles > 1:
        up(1)
    act(0)
    for j in range(n_tiles - 1):
        if j + 2 < n_tiles:
            up(j + 2)
        act(j + 1)
    o_ref[0] = x + mod_ref[0, 5:6, :] * _dot(act_s[...], wdn_ref[...])


def _ffn(x, mod, norm2_g, w_up, conv_w, conv_b, w_down, tm=512, tf=MXU_TILE):
    bsz, s, d = x.shape
    dff = w_down.shape[0]
    pad = -dff % tf
    halves = lambda a: jnp.concatenate(
        [jnp.pad(a[..., :dff], ((0, 0), (0, pad))), jnp.pad(a[..., dff:], ((0, 0), (0, pad)))], axis=-1)
    w_up = halves(w_up).astype(BF16)
    conv_w = halves(conv_w)
    conv_b = halves(conv_b)
    w_down = jnp.pad(w_down, ((0, pad), (0, 0))).astype(BF16)
    hb = tm // CONV_HALO
    full = lambda a: pl.BlockSpec(a.shape, lambda b, i: (0, 0), pipeline_mode=pl.Buffered(1))
    return pl.pallas_call(
        functools.partial(_ffn_kernel, tf=tf),
        grid=(bsz, s // tm),
        in_specs=[
            pl.BlockSpec((1, tm, d), lambda b, i: (b, i, 0)),
            pl.BlockSpec((1, CONV_HALO, d), lambda b, i: (b, jnp.maximum(i * hb - 1, 0), 0)),
            pl.BlockSpec((1, 6, d), lambda b, i: (b, 0, 0)),
            pl.BlockSpec((1, d), lambda b, i: (0, 0)),
            full(w_up), full(conv_w), full(conv_b), full(w_down),
        ],
        out_specs=pl.BlockSpec((1, tm, d), lambda b, i: (b, i, 0)),
        out_shape=jax.ShapeDtypeStruct((bsz, s, d), F32),
        scratch_shapes=[pltpu.VMEM((tm, dff + pad), BF16)]
        + [pltpu.VMEM((CONV_HALO + tm, tf), F32) for _ in range(4)],
        compiler_params=_params(("arbitrary", "arbitrary")),
        name="conv_ffn",
    )(x, x, mod, norm2_g, w_up, conv_w, conv_b, w_down)


def _layer(x, c, w_ada, b_ada, norm1_g, w_in, b_gate, rwkv_mu, rwkv_w0, rwkv_w2, rwkv_a0,
           rwkv_a2, rwkv_g2, rwkv_k_k, rwkv_k_a, rwkv_r_k, rwkv_lnx_g, rwkv_lnx_b, sb_q_g,
           sb_k_g, w_o_rwkv, w_o_sb, w_out, norm2_g, w_up, conv_w, conv_b, w_down):
    bsz, s, d = x.shape
    w = MIX_WIDTH
    rwkv_in = 3 * w + DECAY_LORA + ICLR_LORA + GATE_LORA
    sb_in = 3 * w
    pad = RWKV_PAD - rwkv_in

    w_all = jnp.concatenate([
        w_in[:, :rwkv_in], jnp.zeros((d, pad), F32),
        w_in[:, rwkv_in + sb_in:], w_in[:, rwkv_in:rwkv_in + sb_in]], axis=1).astype(BF16)
    mu = jnp.concatenate([rwkv_mu, jnp.zeros((pad,), F32)]).reshape(1, RWKV_PAD)
    wl = jnp.zeros((DECAY_LORA + ICLR_LORA, 2 * w), F32)
    wl = wl.at[:DECAY_LORA, :w].set(rwkv_w2).at[DECAY_LORA:, w:].set(rwkv_a2).astype(BF16)
    wg = jnp.zeros((256, w), F32).at[:GATE_LORA].set(rwkv_g2).astype(BF16)
    head_id = jnp.arange(w) // HEAD_DIM
    bd_ones = (head_id[:, None] == head_id[None, :]).astype(BF16)
    row = lambda a: a.reshape(1, -1)

    mod = _modulation(c, w_ada, b_ada).reshape(bsz, 6, d)
    f_rwkv, gates, q, k, v = _in_projection(
        x, mod, row(norm1_g), w_all, bd_ones,
        row(jnp.tile(sb_q_g, N_HEADS)), row(jnp.tile(sb_k_g, N_HEADS)))
    y_rwkv, y_sb = _token_mixers(
        f_rwkv, mu, row(rwkv_w0), row(rwkv_a0), row(rwkv_k_k), row(rwkv_k_a), row(rwkv_r_k),
        row(rwkv_lnx_g), row(rwkv_lnx_b), wl, wg, bd_ones, q, k, v)
    x = _merge(x, y_rwkv, y_sb, gates, row(b_gate), mod,
               w_o_rwkv.astype(BF16), w_o_sb.astype(BF16), w_out.astype(BF16))
    return _ffn(x, mod, row(norm2_g), w_up, conv_w, row(conv_b), w_down)


def kernel(x, c, w_ada, b_ada, norm1_g, w_in, b_gate, rwkv_mu, rwkv_w0, rwkv_w2, rwkv_a0, rwkv_a2, rwkv_g2, rwkv_k_k, rwkv_k_a, rwkv_r_k, rwkv_lnx_g, rwkv_lnx_b, sb_q_g, sb_k_g, w_o_rwkv, w_o_sb, w_out, norm2_g, w_up, conv_w, conv_b, w_down):
    params = (w_ada, b_ada, norm1_g, w_in, b_gate, rwkv_mu, rwkv_w0, rwkv_w2, rwkv_a0, rwkv_a2,
              rwkv_g2, rwkv_k_k, rwkv_k_a, rwkv_r_k, rwkv_lnx_g, rwkv_lnx_b, sb_q_g, sb_k_g,
              w_o_rwkv, w_o_sb, w_out, norm2_g, w_up, conv_w, conv_b, w_down)
    for layer in range(w_ada.shape[0]):
        x = _layer(x, c, *(p[layer] for p in params))
    return x
```

```python
import functools

import jax
import jax.numpy as jnp
from jax import lax
from jax.experimental import pallas as pl
from jax.experimental.pallas import tpu as pltpu

F32 = jnp.float32
BF16 = jnp.bfloat16

HEAD_DIM = 64
N_HEADS = 8
MIX_WIDTH = HEAD_DIM * N_HEADS
DECAY_LORA = 64
ICLR_LORA = 64
GATE_LORA = 160
RWKV_PAD = 2048
CONV_WIDTH = 3
CONV_HALO = 8
RMS_EPS = 1e-6
GN_EPS = 64e-5
L2_EPS = 1e-12
CHUNK = 64
LOG2E = 1.4426950408889634
MASKED_SCORE = -1e30
SB_UNROLL = 4
MXU_TILE = 256

VMEM_LIMIT = 56 * 1024 * 1024


def _dot(a, b):
    return jnp.dot(a, b, preferred_element_type=F32)


def _dot_nt(a, b):
    return lax.dot_general(a, b, (((1,), (1,)), ((), ())), preferred_element_type=F32)


def _bmm(a, b):
    return lax.dot_general(a, b, (((2,), (1,)), ((0,), (0,))), preferred_element_type=F32)


def _bmm_nt(a, b):
    return lax.dot_general(a, b, (((2,), (2,)), ((0,), (0,))), preferred_element_type=F32)


def _bmm_tn(a, b):
    return lax.dot_general(a, b, (((1,), (1,)), ((0,), (0,))), preferred_element_type=F32)


def _sigmoid(x):
    return 1.0 / (1.0 + jnp.exp(-x))


def _softplus(x):
    return jnp.maximum(x, 0.0) + jnp.log(1.0 + jnp.exp(-jnp.abs(x)))


def _params(sem):
    return pltpu.CompilerParams(dimension_semantics=sem, vmem_limit_bytes=VMEM_LIMIT)


def _mod_kernel(c_ref, w_ref, b_ref, o_ref):
    c = c_ref[...]
    sc = (c * _sigmoid(c)).astype(BF16)
    o_ref[...] = _dot(sc, w_ref[...].astype(BF16)) + b_ref[...]


def _modulation(c, w_ada, b_ada, tn=1536):
    bsz, d = c.shape
    n = w_ada.shape[1]
    return pl.pallas_call(
        _mod_kernel,
        grid=(n // tn,),
        in_specs=[
            pl.BlockSpec((bsz, d), lambda j: (0, 0)),
            pl.BlockSpec((d, tn), lambda j: (0, j)),
            pl.BlockSpec((1, tn), lambda j: (0, j)),
        ],
        out_specs=pl.BlockSpec((bsz, tn), lambda j: (0, j)),
        out_shape=jax.ShapeDtypeStruct((bsz, n), F32),
        compiler_params=_params(("arbitrary",)),
        name="adaln_mod",
    )(c, w_ada, b_ada.reshape(1, n))


def _inproj_kernel(x_ref, mod_ref, g_ref, w_ref, bd_ref, qg_ref, kg_ref,
                   rw_ref, gate_ref, q_ref, k_ref, v_ref):
    x = x_ref[0]
    ms = jnp.mean(x * x, axis=-1, keepdims=True)
    h = x * lax.rsqrt(ms + RMS_EPS) * g_ref[...]
    h = h * (1.0 + mod_ref[0, 1:2, :]) + mod_ref[0, 0:1, :]
    hb = h.astype(BF16)
    cw = MIX_WIDTH
    for j in range(RWKV_PAD // cw):
        rw_ref[0, :, j * cw:(j + 1) * cw] = _dot(hb, w_ref[:, j * cw:(j + 1) * cw])
    off = RWKV_PAD
    for j in range(2 * x.shape[1] // cw):
        gate_ref[0, :, j * cw:(j + 1) * cw] = _dot(hb, w_ref[:, off + j * cw:off + (j + 1) * cw])
    off += 2 * x.shape[1]

    def head_norm(t, gain):
        msq = _dot((t * t).astype(BF16), bd_ref[...]) * (1.0 / HEAD_DIM)
        return t * lax.rsqrt(msq + RMS_EPS) * gain

    q = head_norm(_dot(hb, w_ref[:, off:off + cw]), qg_ref[...]) * (HEAD_DIM ** -0.5)
    k = head_norm(_dot(hb, w_ref[:, off + cw:off + 2 * cw]), kg_ref[...])
    q_ref[0] = q.astype(BF16)
    k_ref[0] = k.astype(BF16)
    v_ref[0] = _dot(hb, w_ref[:, off + 2 * cw:off + 3 * cw]).astype(BF16)


def _in_projection(x, mod, norm1_g, w_all, bd_ones, qg, kg, tm=512):
    bsz, s, d = x.shape
    n_all = w_all.shape[1]
    head_shape = jax.ShapeDtypeStruct((bsz, s, MIX_WIDTH), BF16)
    head_spec = pl.BlockSpec((1, tm, MIX_WIDTH), lambda b, i: (b, i, 0))
    return pl.pallas_call(
        _inproj_kernel,
        grid=(bsz, s // tm),
        in_specs=[
            pl.BlockSpec((1, tm, d), lambda b, i: (b, i, 0)),
            pl.BlockSpec((1, 6, d), lambda b, i: (b, 0, 0)),
            pl.BlockSpec((1, d), lambda b, i: (0, 0)),
            pl.BlockSpec((d, n_all), lambda b, i: (0, 0), pipeline_mode=pl.Buffered(1)),
            pl.BlockSpec((MIX_WIDTH, MIX_WIDTH), lambda b, i: (0, 0)),
            pl.BlockSpec((1, MIX_WIDTH), lambda b, i: (0, 0)),
            pl.BlockSpec((1, MIX_WIDTH), lambda b, i: (0, 0)),
        ],
        out_specs=[
            pl.BlockSpec((1, tm, RWKV_PAD), lambda b, i: (b, i, 0)),
            pl.BlockSpec((1, tm, 2 * d), lambda b, i: (b, i, 0)),
            head_spec, head_spec, head_spec,
        ],
        out_shape=[
            jax.ShapeDtypeStruct((bsz, s, RWKV_PAD), F32),
            jax.ShapeDtypeStruct((bsz, s, 2 * d), F32),
            head_shape, head_shape, head_shape,
        ],
        compiler_params=_params(("arbitrary", "arbitrary")),
        name="in_proj",
    )(x, mod, norm1_g, w_all, bd_ones, qg, kg)


def _rwkv_kernel(f_ref, mu_ref, w0_ref, a0_ref, kkw_ref, kaw_ref, rk_ref, lng_ref, lnb_ref,
                 wl_ref, wg_ref, bd_ref, cum_ref,
                 y_ref,
                 prev_s, h_s, at_s, rt_s, bt_s, kt_s, bh_s, kh_s, v_s, gc_s, yn_s):
    ts = f_ref.shape[1]
    n_chunks = ts // CHUNK
    w = MIX_WIDTH

    @pl.when(pl.program_id(1) == 0)
    def _():
        prev_s[...] = jnp.zeros_like(prev_s)
        h_s[...] = jnp.zeros_like(h_s)

    f = f_ref[0]
    row = lax.broadcasted_iota(jnp.int32, f.shape, 0)
    shifted = jnp.where(row == 0, prev_s[...], pltpu.roll(f, 1, axis=0))
    prev_s[...] = f[ts - 1:ts, :]
    f = f + (shifted - f) * mu_ref[...]

    r = f[:, 0:w]
    k = f[:, w:2 * w]
    v = f[:, 2 * w:3 * w]
    lora_in = f[:, 3 * w:3 * w + 128]
    gate_in = f[:, 3 * w + 128:3 * w + 384]
    lane = lax.broadcasted_iota(jnp.int32, lora_in.shape, 1)
    lora_act = jnp.where(lane < DECAY_LORA, jnp.tanh(lora_in), lora_in)
    lora = _dot(lora_act.astype(BF16), wl_ref[...])
    g = _dot(_sigmoid(gate_in).astype(BF16), wg_ref[...])

    w_log = -_softplus(-(w0_ref[...] + lora[:, 0:w])) - 0.5
    ld = -jnp.exp(w_log)
    a = _sigmoid(a0_ref[...] + lora[:, w:2 * w])
    kk = k * kkw_ref[...]
    ss = _dot((kk * kk).astype(BF16), bd_ref[...])
    kkn = kk * lax.rsqrt(jnp.maximum(ss, L2_EPS * L2_EPS))
    k2 = k * (1.0 + (a - 1.0) * kaw_ref[...])
    b = kkn * a
    bonus = _dot((r * k2 * rk_ref[...]).astype(BF16), bd_ref[...]) * v

    ld_hi = ld.astype(BF16)
    ld_lo = (ld - ld_hi.astype(F32)).astype(BF16)
    cl = _dot(cum_ref[...], ld_hi) + _dot(cum_ref[...], ld_lo)
    cle = jnp.concatenate([jnp.broadcast_to(cl[(c + 1) * CHUNK - 1:(c + 1) * CHUNK, :], (CHUNK, w))
                           for c in range(n_chunks)], axis=0)
    e_neg = jnp.exp(-cl)
    e_end = jnp.exp(cle - cl)
    at_s[...] = (-kkn * jnp.exp(cl - ld)).astype(BF16)
    rt_s[...] = (r * jnp.exp(cl)).astype(BF16)
    bt_s[...] = (b * e_neg).astype(BF16)
    kt_s[...] = (k2 * e_neg).astype(BF16)
    bh_s[...] = (b * e_end).astype(BF16)
    kh_s[...] = (k2 * e_end).astype(BF16)
    v_s[...] = v.astype(BF16)
    gc_s[...] = jnp.exp(cle)

    ri = lax.broadcasted_iota(jnp.int32, (CHUNK, CHUNK), 0)
    ci = lax.broadcasted_iota(jnp.int32, (CHUNK, CHUNK), 1)
    strict = ri > ci
    incl = ri >= ci
    eye = ri == ci

    def blocks(ref):
        return jnp.stack([ref[c * CHUNK:(c + 1) * CHUNK, hd * HEAD_DIM:(hd + 1) * HEAD_DIM]
                          for c in range(n_chunks) for hd in range(N_HEADS)])

    at, rt, bt, kt, bh, kh, vv = (blocks(s) for s in (at_s, rt_s, bt_s, kt_s, bh_s, kh_s, v_s))
    a_ab = jnp.where(strict, _bmm_nt(at, bt), 0.0)
    a_ak = jnp.where(strict, _bmm_nt(at, kt), 0.0).astype(BF16)
    a_rb = jnp.where(incl, _bmm_nt(rt, bt), 0.0).astype(BF16)
    a_rk = jnp.where(incl, _bmm_nt(rt, kt), 0.0).astype(BF16)
    t = jnp.where(eye, 1.0, a_ab)
    p = a_ab
    for _ in range(5):
        pb = p.astype(BF16)
        p = _bmm(pb, pb)
        t = t + _bmm(t.astype(BF16), p.astype(BF16))
    tb = t.astype(BF16)
    p1 = _bmm(tb, at).astype(BF16)
    p2 = _bmm(tb, _bmm(a_ak, vv).astype(BF16)).astype(BF16)
    q1 = (rt.astype(F32) + _bmm(a_rb, p1)).astype(BF16)
    q2 = _bmm(a_rb, p2) + _bmm(a_rk, vv)
    gc = jnp.stack([gc_s[c * CHUNK:c * CHUNK + 1, hd * HEAD_DIM:(hd + 1) * HEAD_DIM]
                    for c in range(n_chunks) for hd in range(N_HEADS)])
    m = (jnp.where(eye, gc, 0.0) + _bmm_tn(bh, p1)).astype(BF16)
    n = _bmm_tn(jnp.concatenate([bh, kh], axis=1), jnp.concatenate([p2, vv], axis=1))

    h = h_s[...]
    for c in range(n_chunks):
        sl = slice(c * N_HEADS, (c + 1) * N_HEADS)
        hb = h.astype(BF16)
        y = _bmm(q1[sl], hb) + q2[sl]
        h = _bmm(m[sl], hb) + n[sl]
        mean = jnp.mean(y, axis=-1, keepdims=True)
        yc = y - mean
        var = jnp.mean(yc * yc, axis=-1, keepdims=True)
        yn = yc * lax.rsqrt(var + GN_EPS)
        for hd in range(N_HEADS):
            yn_s[c * CHUNK:(c + 1) * CHUNK, hd * HEAD_DIM:(hd + 1) * HEAD_DIM] = yn[hd]
    h_s[...] = h

    y = (yn_s[...] * lng_ref[...] + lnb_ref[...] + bonus) * g
    y_ref[0] = y.astype(BF16)


def _sb_kernel(qi, q_ref, k_ref, v_ref, suffix_ref, o_ref, z_s, sp_s, acc_s, carry_s):
    tq = q_ref.shape[1]
    tk = suffix_ref.shape[0]
    ratio = tq // tk
    suffix = suffix_ref[...]
    lane = lax.broadcasted_iota(jnp.int32, (tq, 2 * HEAD_DIM), 1)
    qp = q_ref[0]
    zero = jnp.zeros_like(qp)
    q_heads = (jnp.where(lane < HEAD_DIM, qp, zero), jnp.where(lane >= HEAD_DIM, qp, zero))

    def key_block(n):
        j = jnp.maximum(qi * ratio + (ratio - 1) - n, 0)
        return j, pl.ds(pl.multiple_of(j * tk, tk), tk)

    def first_row(n):
        return (ratio - 1 - n) * tk if is_own(n) else 0

    def is_own(n):
        return isinstance(n, int) and n < ratio

    def scores(n, slot):
        j, rows = key_block(n)
        r0 = first_row(n)
        kb = k_ref[0, rows, :]
        z_both = _dot_nt(jnp.concatenate([q_heads[0][r0:, :], q_heads[1][r0:, :]], axis=0), kb)
        nr = tq - r0
        for hd in range(2):
            z = z_both[hd * nr:(hd + 1) * nr, :]
            zb = z.astype(BF16)
            sp = jnp.maximum(zb, 0.0) + jnp.log(1.0 + jnp.exp2(jnp.abs(zb) * (-LOG2E)))
            if is_own(n):
                ri = lax.broadcasted_iota(jnp.int32, z.shape, 0)
                ci = lax.broadcasted_iota(jnp.int32, z.shape, 1)
                causal = ci < ri
                sp = jnp.where(causal, sp, jnp.zeros_like(sp))
                z = jnp.where(causal, z, MASKED_SCORE)
            z_s[hd, slot, r0:, :] = z
            sp_s[hd, slot, r0:, :] = sp

    def accumulate(n, slot):
        _, rows = key_block(n)
        r0 = first_row(n)
        vb = v_ref[0, rows, :]
        nr = tq - r0
        cs_both = _dot(jnp.concatenate([sp_s[0, slot, r0:, :], sp_s[1, slot, r0:, :]], axis=0), suffix)
        for hd in range(2):
            cs = cs_both[hd * nr:(hd + 1) * nr, :]
            carry = carry_s[hd, r0:, :]
            a = jnp.exp(z_s[hd, slot, r0:, :] - cs - jnp.concatenate([carry] * (tk // 128), axis=1))
            acc_s[hd, r0:, :] += _dot(a.astype(BF16), vb)
            carry_s[hd, r0:, :] = carry + jnp.broadcast_to(cs[:, 0:1], carry.shape)

    acc_s[...] = jnp.zeros_like(acc_s)
    carry_s[...] = jnp.zeros_like(carry_s)
    scores(0, 0)
    for n in range(ratio):
        scores(n + 1, (n + 1) % 2)
        accumulate(n, n % 2)

    def group(p, _):
        n0 = ratio + SB_UNROLL * p
        for u in range(SB_UNROLL):
            scores(n0 + u + 1, (u + 1) % 2)
            accumulate(n0 + u, u % 2)
        return 0

    lax.fori_loop(0, qi * (ratio // SB_UNROLL), group, 0)
    o_ref[0] = jnp.where(lane < HEAD_DIM, acc_s[0], acc_s[1]).astype(BF16)


N_RWKV_IN = 13
N_RWKV_SCRATCH = 11


def _mix_kernel(*refs):
    rw_in, sb_in = refs[:N_RWKV_IN], refs[N_RWKV_IN:N_RWKV_IN + 4]
    y_ref, o_ref = refs[N_RWKV_IN + 4:N_RWKV_IN + 6]
    scratch = refs[N_RWKV_IN + 6:]
    n_q = sb_in[1].shape[1] // sb_in[0].shape[1]
    _rwkv_kernel(*rw_in, y_ref, *scratch[:N_RWKV_SCRATCH])
    _sb_kernel(pl.program_id(1) % n_q, *sb_in, o_ref, *scratch[N_RWKV_SCRATCH:])


def _token_mixers(f_rwkv, mu, w0, a0, k_k, k_a, r_k, lnx_g, lnx_b, wl, wg, bd_ones, q, k, v,
                  ts=256, tq=1024, tk=256):
    bsz, s, fw = f_rwkv.shape
    w = MIX_WIDTH
    pw = 2 * HEAD_DIM
    n_q = s // tq
    assert s // ts == (w // pw) * n_q
    assert (tq // tk) % SB_UNROLL == 0 and SB_UNROLL % 2 == 0
    ri = jnp.arange(ts)[:, None]
    ci = jnp.arange(ts)[None, :]
    same = (ri // CHUNK) == (ci // CHUNK)
    cum = (same & (ci <= ri)).astype(BF16)
    suffix =(jnp.arange(tk)[:, None] >= jnp.arange(tk)[None, :]).astype(BF16)
    vec = lambda n: pl.BlockSpec((1, n), lambda b, i: (0, 0))
    full = lambda a: pl.BlockSpec(a.shape, lambda b, i: (0, 0))
    return pl.pallas_call(
        _mix_kernel,
        grid=(bsz, s // ts),
        in_specs=[
            pl.BlockSpec((1, ts, fw), lambda b, i: (b, i, 0)),
            vec(fw), vec(w), vec(w), vec(w), vec(w), vec(w), vec(w), vec(w),
            full(wl), full(wg), full(bd_ones), full(cum),
            pl.BlockSpec((1, tq, pw), lambda b, i: (b, i % n_q, i // n_q)),
            pl.BlockSpec((1, s, pw), lambda b, i: (b, 0, i // n_q)),
            pl.BlockSpec((1, s, pw), lambda b, i: (b, 0, i // n_q)),
            pl.BlockSpec((tk, tk), lambda b, i: (0, 0)),
        ],
        out_specs=[
            pl.BlockSpec((1, ts, w), lambda b, i: (b, i, 0)),
            pl.BlockSpec((1, tq, pw), lambda b, i: (b, i % n_q, i // n_q)),
        ],
        out_shape=[jax.ShapeDtypeStruct((bsz, s, w), BF16), jax.ShapeDtypeStruct((bsz, s, w), BF16)],
        scratch_shapes=[
            pltpu.VMEM((1, fw), F32),
            pltpu.VMEM((N_HEADS, HEAD_DIM, HEAD_DIM), F32),
            pltpu.VMEM((ts, w), BF16), pltpu.VMEM((ts, w), BF16), pltpu.VMEM((ts, w), BF16),
            pltpu.VMEM((ts, w), BF16), pltpu.VMEM((ts, w), BF16), pltpu.VMEM((ts, w), BF16),
            pltpu.VMEM((ts, w), BF16),
            pltpu.VMEM((ts, w), F32),
            pltpu.VMEM((ts, w), F32),
            pltpu.VMEM((2, 2, tq, tk), F32),
            pltpu.VMEM((2, 2, tq, tk), BF16),
            pltpu.VMEM((2, tq, pw), F32),
            pltpu.VMEM((2, tq, pw), F32),
        ],
        compiler_params=_params(("arbitrary", "arbitrary")),
        name="token_mixers",
    )(f_rwkv, mu, w0, a0, k_k, k_a, r_k, lnx_g, lnx_b, wl, wg, bd_ones, cum, q, k, v, suffix)


def _merge_kernel(x_ref, yr_ref, ys_ref, gate_ref, bg_ref, mod_ref, wor_ref, wos_ref, wout_ref, o_ref):
    d = x_ref.shape[2]
    gr = _sigmoid(gate_ref[0, :, 0:d] + bg_ref[:, 0:d])
    gs = _sigmoid(gate_ref[0, :, d:2 * d] + bg_ref[:, d:2 * d])
    merged = gr * _dot(yr_ref[0], wor_ref[...]) + gs * _dot(ys_ref[0], wos_ref[...])
    o_ref[0] = x_ref[0] + mod_ref[0, 2:3, :] * _dot(merged.astype(BF16), wout_ref[...])


def _merge(x, y_rwkv, y_sb, gates, b_gate, mod, w_or, w_os, w_out, tm=512):
    bsz, s, d = x.shape
    full = lambda a: pl.BlockSpec(a.shape, lambda b, i: (0, 0))
    return pl.pallas_call(
        _merge_kernel,
        grid=(bsz, s // tm),
        in_specs=[
            pl.BlockSpec((1, tm, d), lambda b, i: (b, i, 0)),
            pl.BlockSpec((1, tm, MIX_WIDTH), lambda b, i: (b, i, 0)),
            pl.BlockSpec((1, tm, MIX_WIDTH), lambda b, i: (b, i, 0)),
            pl.BlockSpec((1, tm, 2 * d), lambda b, i: (b, i, 0)),
            pl.BlockSpec((1, 2 * d), lambda b, i: (0, 0)),
            pl.BlockSpec((1, 6, d), lambda b, i: (b, 0, 0)),
            full(w_or), full(w_os), full(w_out),
        ],
        out_specs=pl.BlockSpec((1, tm, d), lambda b, i: (b, i, 0)),
        out_shape=jax.ShapeDtypeStruct((bsz, s, d), F32),
        compiler_params=_params(("arbitrary", "arbitrary")),
        name="merge_out",
    )(x, y_rwkv, y_sb, gates, b_gate, mod, w_or, w_os, w_out)


def _ffn_kernel(x_ref, halo_ref, mod_ref, g_ref, wup_ref, cw_ref, cb_ref, wdn_ref, o_ref, act_s, *u_bufs, tf):
    tm = x_ref.shape[1]
    dff = wdn_ref.shape[0]
    x = x_ref[0]

    def norm_mod(t):
        ms = jnp.mean(t * t, axis=-1, keepdims=True)
        hh = t * lax.rsqrt(ms + RMS_EPS) * g_ref[...]
        return hh * (1.0 + mod_ref[0, 4:5, :]) + mod_ref[0, 3:4, :]

    h = norm_mod(x).astype(BF16)
    keep = (pl.program_id(1) > 0).astype(F32)
    hh = (norm_mod(halo_ref[0]) * keep).astype(BF16)
    hcat = jnp.concatenate([hh, h], axis=0)

    n_tiles = dff // tf

    def up(j):
        for half in range(2):
            col0 = half * dff + j * tf
            u_bufs[(2 * j + half) % len(u_bufs)][...] = _dot(hcat, wup_ref[:, col0:col0 + tf])

    def conv(j, half):
        cols = slice(half * dff + j * tf, half * dff + (j + 1) * tf)
        u_s = u_bufs[(2 * j + half) % len(u_bufs)]
        cw = cw_ref[:, cols]
        out = cb_ref[:, cols] + cw[2:3, :] * u_s[CONV_HALO:, :]
        out = out + cw[1:2, :] * u_s[CONV_HALO - 1:CONV_HALO - 1 + tm, :]
        out = out + cw[0:1, :] * u_s[CONV_HALO - 2:CONV_HALO - 2 + tm, :]
        return out

    def act(j):
        gt = conv(j, 1)
        act_s[:, j * tf:(j + 1) * tf] = (gt * _sigmoid(gt) * conv(j, 0)).astype(BF16)

    up(0)
    if n_tiles > 1:
        up(1)
    act(0)
    for j in range(n_tiles - 1):
        if j + 2 < n_tiles:
            up(j + 2)
        act(j + 1)
    o_ref[0] = x + mod_ref[0, 5:6, :] * _dot(act_s[...], wdn_ref[...])


def _ffn(x, mod, norm2_g, w_up, conv_w, conv_b, w_down, tm=512, tf=MXU_TILE):
    bsz, s, d = x.shape
    dff = w_down.shape[0]
    pad = -dff % tf
    halves = lambda a: jnp.concatenate(
        [jnp.pad(a[..., :dff], ((0, 0), (0, pad))), jnp.pad(a[..., dff:], ((0, 0), (0, pad)))], axis=-1)
    w_up = halves(w_up).astype(BF16)
    conv_w = halves(conv_w)
    conv_b = halves(conv_b)
    w_down = jnp.pad(w_down, ((0, pad), (0, 0))).astype(BF16)
    hb = tm // CONV_HALO
    full = lambda a: pl.BlockSpec(a.shape, lambda b, i: (0, 0), pipeline_mode=pl.Buffered(1))
    return pl.pallas_call(
        functools.partial(_ffn_kernel, tf=tf),
        grid=(bsz, s // tm),
        in_specs=[
            pl.BlockSpec((1, tm, d), lambda b, i: (b, i, 0)),
            pl.BlockSpec((1, CONV_HALO, d), lambda b, i: (b, jnp.maximum(i * hb - 1, 0), 0)),
            pl.BlockSpec((1, 6, d), lambda b, i: (b, 0, 0)),
            pl.BlockSpec((1, d), lambda b, i: (0, 0)),
            full(w_up), full(conv_w), full(conv_b), full(w_down),
        ],
        out_specs=pl.BlockSpec((1, tm, d), lambda b, i: (b, i, 0)),
        out_shape=jax.ShapeDtypeStruct((bsz, s, d), F32),
        scratch_shapes=[pltpu.VMEM((tm, dff + pad), BF16)]
        + [pltpu.VMEM((CONV_HALO + tm, tf), F32) for _ in range(4)],
        compiler_params=_params(("arbitrary", "arbitrary")),
        name="conv_ffn",
    )(x, x, mod, norm2_g, w_up, conv_w, conv_b, w_down)


def _layer(x, c, w_ada, b_ada, norm1_g, w_in, b_gate, rwkv_mu, rwkv_w0, rwkv_w2, rwkv_a0,
           rwkv_a2, rwkv_g2, rwkv_k_k, rwkv_k_a, rwkv_r_k, rwkv_lnx_g, rwkv_lnx_b, sb_q_g,
           sb_k_g, w_o_rwkv, w_o_sb, w_out, norm2_g, w_up, conv_w, conv_b, w_down):
    bsz, s, d = x.shape
    w = MIX_WIDTH
    rwkv_in = 3 * w + DECAY_LORA + ICLR_LORA + GATE_LORA
    sb_in = 3 * w
    pad = RWKV_PAD - rwkv_in

    w_all = jnp.concatenate([
        w_in[:, :rwkv_in], jnp.zeros((d, pad), F32),
        w_in[:, rwkv_in + sb_in:], w_in[:, rwkv_in:rwkv_in + sb_in]], axis=1).astype(BF16)
    mu = jnp.concatenate([rwkv_mu, jnp.zeros((pad,), F32)]).reshape(1, RWKV_PAD)
    wl = jnp.zeros((DECAY_LORA + ICLR_LORA, 2 * w), F32)
    wl = wl.at[:DECAY_LORA, :w].set(rwkv_w2).at[DECAY_LORA:, w:].set(rwkv_a2).astype(BF16)
    wg = jnp.zeros((256, w), F32).at[:GATE_LORA].set(rwkv_g2).astype(BF16)
    head_id = jnp.arange(w) // HEAD_DIM
    bd_ones = (head_id[:, None] == head_id[None, :]).astype(BF16)
    row = lambda a: a.reshape(1, -1)

    mod = _modulation(c, w_ada, b_ada).reshape(bsz, 6, d)
    f_rwkv, gates, q, k, v = _in_projection(
        x, mod, row(norm1_g), w_all, bd_ones,
        row(jnp.tile(sb_q_g, N_HEADS)), row(jnp.tile(sb_k_g, N_HEADS)))
    y_rwkv, y_sb = _token_mixers(
        f_rwkv, mu, row(rwkv_w0), row(rwkv_a0), row(rwkv_k_k), row(rwkv_k_a), row(rwkv_r_k),
        row(rwkv_lnx_g), row(rwkv_lnx_b), wl, wg, bd_ones, q, k, v)
    x = _merge(x, y_rwkv, y_sb, gates, row(b_gate), mod,
               w_o_rwkv.astype(BF16), w_o_sb.astype(BF16), w_out.astype(BF16))
    return _ffn(x, mod, row(norm2_g), w_up, conv_w, row(conv_b), w_down)


def kernel(x, c, w_ada, b_ada, norm1_g, w_in, b_gate, rwkv_mu, rwkv_w0, rwkv_w2, rwkv_a0, rwkv_a2, rwkv_g2, rwkv_k_k, rwkv_k_a, rwkv_r_k, rwkv_lnx_g, rwkv_lnx_b, sb_q_g, sb_k_g, w_o_rwkv, w_o_sb, w_out, norm2_g, w_up, conv_w, conv_b, w_down):
    params = (w_ada, b_ada, norm1_g, w_in, b_gate, rwkv_mu, rwkv_w0, rwkv_w2, rwkv_a0, rwkv_a2,
              rwkv_g2, rwkv_k_k, rwkv_k_a, rwkv_r_k, rwkv_lnx_g, rwkv_lnx_b, sb_q_g, sb_k_g,
              w_o_rwkv, w_o_sb, w_out, norm2_g, w_up, conv_w, conv_b, w_down)
    for layer in range(w_ada.shape[0]):
        x = _layer(x, c, *(p[layer] for p in params))
    return x
```

```python
import functools

import jax
import jax.numpy as jnp
from jax import lax
from jax.experimental import pallas as pl
from jax.experimental.pallas import tpu as pltpu

F32 = jnp.float32
BF16 = jnp.bfloat16

HEAD_DIM = 64
N_HEADS = 8
MIX_WIDTH = HEAD_DIM * N_HEADS
DECAY_LORA = 64
ICLR_LORA = 64
GATE_LORA = 160
RWKV_PAD = 2048
CONV_WIDTH = 3
CONV_HALO = 8
RMS_EPS = 1e-6
GN_EPS = 64e-5
L2_EPS = 1e-12
CHUNK = 64
LOG2E = 1.4426950408889634
MASKED_SCORE = -1e30
SB_UNROLL = 4
MXU_TILE = 256

VMEM_LIMIT = 56 * 1024 * 1024


def _dot(a, b):
    return jnp.dot(a, b, preferred_element_type=F32)


def _dot_nt(a, b):
    return lax.dot_general(a, b, (((1,), (1,)), ((), ())), preferred_element_type=F32)


def _bmm(a, b):
    return lax.dot_general(a, b, (((2,), (1,)), ((0,), (0,))), preferred_element_type=F32)


def _bmm_nt(a, b):
    return lax.dot_general(a, b, (((2,), (2,)), ((0,), (0,))), preferred_element_type=F32)


def _bmm_tn(a, b):
    return lax.dot_general(a, b, (((1,), (1,)), ((0,), (0,))), preferred_element_type=F32)


def _sigmoid(x):
    return 1.0 / (1.0 + jnp.exp(-x))


def _softplus(x):
    return jnp.maximum(x, 0.0) + jnp.log(1.0 + jnp.exp(-jnp.abs(x)))


def _params(sem):
    return pltpu.CompilerParams(dimension_semantics=sem, vmem_limit_bytes=VMEM_LIMIT)


def _mod_kernel(c_ref, w_ref, b_ref, o_ref):
    c = c_ref[...]
    sc = (c * _sigmoid(c)).astype(BF16)
    o_ref[...] = _dot(sc, w_ref[...].astype(BF16)) + b_ref[...]


def _modulation(c, w_ada, b_ada, tn=1536):
    bsz, d = c.shape
    n = w_ada.shape[1]
    return pl.pallas_call(
        _mod_kernel,
        grid=(n // tn,),
        in_specs=[
            pl.BlockSpec((bsz, d), lambda j: (0, 0)),
            pl.BlockSpec((d, tn), lambda j: (0, j)),
            pl.BlockSpec((1, tn), lambda j: (0, j)),
        ],
        out_specs=pl.BlockSpec((bsz, tn), lambda j: (0, j)),
        out_shape=jax.ShapeDtypeStruct((bsz, n), F32),
        compiler_params=_params(("arbitrary",)),
        name="adaln_mod",
    )(c, w_ada, b_ada.reshape(1, n))


def _inproj_kernel(x_ref, mod_ref, g_ref, w_ref, bd_ref, qg_ref, kg_ref,
                   rw_ref, gate_ref, q_ref, k_ref, v_ref):
    x = x_ref[0]
    ms = jnp.mean(x * x, axis=-1, keepdims=True)
    h = x * lax.rsqrt(ms + RMS_EPS) * g_ref[...]
    h = h * (1.0 + mod_ref[0, 1:2, :]) + mod_ref[0, 0:1, :]
    hb = h.astype(BF16)
    cw = MIX_WIDTH
    for j in range(RWKV_PAD // cw):
        rw_ref[0, :, j * cw:(j + 1) * cw] = _dot(hb, w_ref[:, j * cw:(j + 1) * cw])
    off = RWKV_PAD
    for j in range(2 * x.shape[1] // cw):
        gate_ref[0, :, j * cw:(j + 1) * cw] = _dot(hb, w_ref[:, off + j * cw:off + (j + 1) * cw])
    off += 2 * x.shape[1]

    def head_norm(t, gain):
        msq = _dot((t * t).astype(BF16), bd_ref[...]) * (1.0 / HEAD_DIM)
        return t * lax.rsqrt(msq + RMS_EPS) * gain

    q = head_norm(_dot(hb, w_ref[:, off:off + cw]), qg_ref[...]) * (HEAD_DIM ** -0.5)
    k = head_norm(_dot(hb, w_ref[:, off + cw:off + 2 * cw]), kg_ref[...])
    q_ref[0] = q.astype(BF16)
    k_ref[0] = k.astype(BF16)
    v_ref[0] = _dot(hb, w_ref[:, off + 2 * cw:off + 3 * cw]).astype(BF16)


def _in_projection(x, mod, norm1_g, w_all, bd_ones, qg, kg, tm=512):
    bsz, s, d = x.shape
    n_all = w_all.shape[1]
    head_shape = jax.ShapeDtypeStruct((bsz, s, MIX_WIDTH), BF16)
    head_spec = pl.BlockSpec((1, tm, MIX_WIDTH), lambda b, i: (b, i, 0))
    return pl.pallas_call(
        _inproj_kernel,
        grid=(bsz, s // tm),
        in_specs=[
            pl.BlockSpec((1, tm, d), lambda b, i: (b, i, 0)),
            pl.BlockSpec((1, 6, d), lambda b, i: (b, 0, 0)),
            pl.BlockSpec((1, d), lambda b, i: (0, 0)),
            pl.BlockSpec((d, n_all), lambda b, i: (0, 0), pipeline_mode=pl.Buffered(1)),
            pl.BlockSpec((MIX_WIDTH, MIX_WIDTH), lambda b, i: (0, 0)),
            pl.BlockSpec((1, MIX_WIDTH), lambda b, i: (0, 0)),
            pl.BlockSpec((1, MIX_WIDTH), lambda b, i: (0, 0)),
        ],
        out_specs=[
            pl.BlockSpec((1, tm, RWKV_PAD), lambda b, i: (b, i, 0)),
            pl.BlockSpec((1, tm, 2 * d), lambda b, i: (b, i, 0)),
            head_spec, head_spec, head_spec,
        ],
        out_shape=[
            jax.ShapeDtypeStruct((bsz, s, RWKV_PAD), F32),
            jax.ShapeDtypeStruct((bsz, s, 2 * d), F32),
            head_shape, head_shape, head_shape,
        ],
        compiler_params=_params(("arbitrary", "arbitrary")),
        name="in_proj",
    )(x, mod, norm1_g, w_all, bd_ones, qg, kg)


def _rwkv_kernel(f_ref, mu_ref, w0_ref, a0_ref, kkw_ref, kaw_ref, rk_ref, lng_ref, lnb_ref,
                 wl_ref, wg_ref, bd_ref, cum_ref,
                 y_ref,
                 prev_s, h_s, at_s, rt_s, bt_s, kt_s, bh_s, kh_s, v_s, gc_s, yn_s):
    ts = f_ref.shape[1]
    n_chunks = ts // CHUNK
    w = MIX_WIDTH

    @pl.when(pl.program_id(1) == 0)
    def _():
        prev_s[...] = jnp.zeros_like(prev_s)
        h_s[...] = jnp.zeros_like(h_s)

    f = f_ref[0]
    row = lax.broadcasted_iota(jnp.int32, f.shape, 0)
    shifted = jnp.where(row == 0, prev_s[...], pltpu.roll(f, 1, axis=0))
    prev_s[...] = f[ts - 1:ts, :]
    f = f + (shifted - f) * mu_ref[...]

    r = f[:, 0:w]
    k = f[:, w:2 * w]
    v = f[:, 2 * w:3 * w]
    lora_in = f[:, 3 * w:3 * w + 128]
    gate_in = f[:, 3 * w + 128:3 * w + 384]
    lane = lax.broadcasted_iota(jnp.int32, lora_in.shape, 1)
    lora_act = jnp.where(lane < DECAY_LORA, jnp.tanh(lora_in), lora_in)
    lora = _dot(lora_act.astype(BF16), wl_ref[...])
    g = _dot(_sigmoid(gate_in).astype(BF16), wg_ref[...])

    w_log = -_softplus(-(w0_ref[...] + lora[:, 0:w])) - 0.5
    ld = -jnp.exp(w_log)
    a = _sigmoid(a0_ref[...] + lora[:, w:2 * w])
    kk = k * kkw_ref[...]
    ss = _dot((kk * kk).astype(BF16), bd_ref[...])
    kkn = kk * lax.rsqrt(jnp.maximum(ss, L2_EPS * L2_EPS))
    k2 = k * (1.0 + (a - 1.0) * kaw_ref[...])
    b = kkn * a
    bonus = _dot((r * k2 * rk_ref[...]).astype(BF16), bd_ref[...]) * v

    ld_hi = ld.astype(BF16)
    ld_lo = (ld - ld_hi.astype(F32)).astype(BF16)
    cl = _dot(cum_ref[...], ld_hi) + _dot(cum_ref[...], ld_lo)
    cle = jnp.concatenate([jnp.broadcast_to(cl[(c + 1) * CHUNK - 1:(c + 1) * CHUNK, :], (CHUNK, w))
                           for c in range(n_chunks)], axis=0)
    e_neg = jnp.exp(-cl)
    e_end = jnp.exp(cle - cl)
    at_s[...] = (-kkn * jnp.exp(cl - ld)).astype(BF16)
    rt_s[...] = (r * jnp.exp(cl)).astype(BF16)
    bt_s[...] = (b * e_neg).astype(BF16)
    kt_s[...] = (k2 * e_neg).astype(BF16)
    bh_s[...] = (b * e_end).astype(BF16)
    kh_s[...] = (k2 * e_end).astype(BF16)
    v_s[...] = v.astype(BF16)
    gc_s[...] = jnp.exp(cle)

    ri = lax.broadcasted_iota(jnp.int32, (CHUNK, CHUNK), 0)
    ci = lax.broadcasted_iota(jnp.int32, (CHUNK, CHUNK), 1)
    strict = ri > ci
    incl = ri >= ci
    eye = ri == ci

    def blocks(ref):
        return jnp.stack([ref[c * CHUNK:(c + 1) * CHUNK, hd * HEAD_DIM:(hd + 1) * HEAD_DIM]
                          for c in range(n_chunks) for hd in range(N_HEADS)])

    at, rt, bt, kt, bh, kh, vv = (blocks(s) for s in (at_s, rt_s, bt_s, kt_s, bh_s, kh_s, v_s))
    a_ab = jnp.where(strict, _bmm_nt(at, bt), 0.0)
    a_ak = jnp.where(strict, _bmm_nt(at, kt), 0.0).astype(BF16)
    a_rb = jnp.where(incl, _bmm_nt(rt, bt), 0.0).astype(BF16)
    a_rk = jnp.where(incl, _bmm_nt(rt, kt), 0.0).astype(BF16)
    t = jnp.where(eye, 1.0, a_ab)
    p = a_ab
    for _ in range(5):
        pb = p.astype(BF16)
        p = _bmm(pb, pb)
        t = t + _bmm(t.astype(BF16), p.astype(BF16))
    tb = t.astype(BF16)
    p1 = _bmm(tb, at).astype(BF16)
    p2 = _bmm(tb, _bmm(a_ak, vv).astype(BF16)).astype(BF16)
    q1 = (rt.astype(F32) + _bmm(a_rb, p1)).astype(BF16)
    q2 = _bmm(a_rb, p2) + _bmm(a_rk, vv)
    gc = jnp.stack([gc_s[c * CHUNK:c * CHUNK + 1, hd * HEAD_DIM:(hd + 1) * HEAD_DIM]
                    for c in range(n_chunks) for hd in range(N_HEADS)])
    m = (jnp.where(eye, gc, 0.0) + _bmm_tn(bh, p1)).astype(BF16)
    n = _bmm_tn(jnp.concatenate([bh, kh], axis=1), jnp.concatenate([p2, vv], axis=1))

    h = h_s[...]
    for c in range(n_chunks):
        sl = slice(c * N_HEADS, (c + 1) * N_HEADS)
        hb = h.astype(BF16)
        y = _bmm(q1[sl], hb) + q2[sl]
        h = _bmm(m[sl], hb) + n[sl]
        mean = jnp.mean(y, axis=-1, keepdims=True)
        yc = y - mean
        var = jnp.mean(yc * yc, axis=-1, keepdims=True)
        yn = yc * lax.rsqrt(var + GN_EPS)
        for hd in range(N_HEADS):
            yn_s[c * CHUNK:(c + 1) * CHUNK, hd * HEAD_DIM:(hd + 1) * HEAD_DIM] = yn[hd]
    h_s[...] = h

    y = (yn_s[...] * lng_ref[...] + lnb_ref[...] + bonus) * g
    y_ref[0] = y.astype(BF16)


def _sb_kernel(qi, q_ref, k_ref, v_ref, suffix_ref, o_ref, z_s, sp_s, acc_s, carry_s):
    tq = q_ref.shape[1]
    tk = suffix_ref.shape[0]
    ratio = tq // tk
    suffix = suffix_ref[...]
    lane = lax.broadcasted_iota(jnp.int32, (tq, 2 * HEAD_DIM), 1)
    qp = q_ref[0]
    zero = jnp.zeros_like(qp)
    q_heads = (jnp.where(lane < HEAD_DIM, qp, zero), jnp.where(lane >= HEAD_DIM, qp, zero))

    def key_block(n):
        j = jnp.maximum(qi * ratio + (ratio - 1) - n, 0)
        return j, pl.ds(pl.multiple_of(j * tk, tk), tk)

    def first_row(n):
        return (ratio - 1 - n) * tk if is_own(n) else 0

    def is_own(n):
        return isinstance(n, int) and n < ratio

    def scores(n, slot):
        j, rows = key_block(n)
        r0 = first_row(n)
        kb = k_ref[0, rows, :]
        for hd in range(2):
            z = _dot_nt(q_heads[hd][r0:, :], kb)
            zb = z.astype(BF16)
            sp = jnp.maximum(zb, 0.0) + jnp.log(1.0 + jnp.exp2(jnp.abs(zb) * (-LOG2E)))
            if is_own(n):
                ri = lax.broadcasted_iota(jnp.int32, z.shape, 0)
                ci = lax.broadcasted_iota(jnp.int32, z.shape, 1)
                causal = ci < ri
                sp = jnp.where(causal, sp, jnp.zeros_like(sp))
                z = jnp.where(causal, z, MASKED_SCORE)
            z_s[hd, slot, r0:, :] = z
            sp_s[hd, slot, r0:, :] = sp

    def accumulate(n, slot):
        _, rows = key_block(n)
        r0 = first_row(n)
        vb = v_ref[0, rows, :]
        for hd in range(2):
            cs = _dot(sp_s[hd, slot, r0:, :], suffix)
            carry = carry_s[hd, r0:, :]
            a = jnp.exp(z_s[hd, slot, r0:, :] - cs - jnp.concatenate([carry] * (tk // 128), axis=1))
            acc_s[hd, r0:, :] += _dot(a.astype(BF16), vb)
            carry_s[hd, r0:, :] = carry + jnp.broadcast_to(cs[:, 0:1], carry.shape)

    acc_s[...] = jnp.zeros_like(acc_s)
    carry_s[...] = jnp.zeros_like(carry_s)
    scores(0, 0)
    for n in range(ratio):
        scores(n + 1, (n + 1) % 2)
        accumulate(n, n % 2)

    def group(p, _):
        n0 = ratio + SB_UNROLL * p
        for u in range(SB_UNROLL):
            scores(n0 + u + 1, (u + 1) % 2)
            accumulate(n0 + u, u % 2)
        return 0

    lax.fori_loop(0, qi * (ratio // SB_UNROLL), group, 0)
    o_ref[0] = jnp.where(lane < HEAD_DIM, acc_s[0], acc_s[1]).astype(BF16)


N_RWKV_IN = 13
N_RWKV_SCRATCH = 11


def _mix_kernel(*refs):
    rw_in, sb_in = refs[:N_RWKV_IN], refs[N_RWKV_IN:N_RWKV_IN + 4]
    y_ref, o_ref = refs[N_RWKV_IN + 4:N_RWKV_IN + 6]
    scratch = refs[N_RWKV_IN + 6:]
    n_q = sb_in[1].shape[1] // sb_in[0].shape[1]
    _rwkv_kernel(*rw_in, y_ref, *scratch[:N_RWKV_SCRATCH])
    _sb_kernel(pl.program_id(1) % n_q, *sb_in, o_ref, *scratch[N_RWKV_SCRATCH:])


def _token_mixers(f_rwkv, mu, w0, a0, k_k, k_a, r_k, lnx_g, lnx_b, wl, wg, bd_ones, q, k, v,
                  ts=256, tq=1024, tk=256):
    bsz, s, fw = f_rwkv.shape
    w = MIX_WIDTH
    pw = 2 * HEAD_DIM
    n_q = s // tq
    assert s // ts == (w // pw) * n_q
    assert (tq // tk) % SB_UNROLL == 0 and SB_UNROLL % 2 == 0
    ri = jnp.arange(ts)[:, None]
    ci = jnp.arange(ts)[None, :]
    same = (ri // CHUNK) == (ci // CHUNK)
    cum = (same & (ci <= ri)).astype(BF16)
    suffix = (jnp.arange(tk)[:, None] >= jnp.arange(tk)[None, :]).astype(BF16)
    vec = lambda n: pl.BlockSpec((1, n), lambda b, i: (0, 0))
    full = lambda a: pl.BlockSpec(a.shape, lambda b, i: (0, 0))
    return pl.pallas_call(
        _mix_kernel,
        grid=(bsz, s // ts),
        in_specs=[
            pl.BlockSpec((1, ts, fw), lambda b, i: (b, i, 0)),
            vec(fw), vec(w), vec(w), vec(w), vec(w), vec(w), vec(w), vec(w),
            full(wl), full(wg), full(bd_ones), full(cum),
            pl.BlockSpec((1, tq, pw), lambda b, i: (b, i % n_q, i // n_q)),
            pl.BlockSpec((1, s, pw), lambda b, i: (b, 0, i // n_q)),
            pl.BlockSpec((1, s, pw), lambda b, i: (b, 0, i // n_q)),
            pl.BlockSpec((tk, tk), lambda b, i: (0, 0)),
        ],
        out_specs=[
            pl.BlockSpec((1, ts, w), lambda b, i: (b, i, 0)),
            pl.BlockSpec((1, tq, pw), lambda b, i: (b, i % n_q, i // n_q)),
        ],
        out_shape=[jax.ShapeDtypeStruct((bsz, s, w), BF16), jax.ShapeDtypeStruct((bsz, s, w), BF16)],
        scratch_shapes=[
            pltpu.VMEM((1, fw), F32),
            pltpu.VMEM((N_HEADS, HEAD_DIM, HEAD_DIM), F32),
            pltpu.VMEM((ts, w), BF16), pltpu.VMEM((ts, w), BF16), pltpu.VMEM((ts, w), BF16),
            pltpu.VMEM((ts, w), BF16), pltpu.VMEM((ts, w), BF16), pltpu.VMEM((ts, w), BF16),
            pltpu.VMEM((ts, w), BF16),
            pltpu.VMEM((ts, w), F32),
            pltpu.VMEM((ts, w), F32),
            pltpu.VMEM((2, 2, tq, tk), F32),
            pltpu.VMEM((2, 2, tq, tk), BF16),
            pltpu.VMEM((2, tq, pw), F32),
            pltpu.VMEM((2, tq, pw), F32),
        ],
        compiler_params=_params(("arbitrary", "arbitrary")),
        name="token_mixers",
    )(f_rwkv, mu, w0, a0, k_k, k_a, r_k, lnx_g, lnx_b, wl, wg, bd_ones, cum, q, k, v, suffix)


def _merge_kernel(x_ref, yr_ref, ys_ref, gate_ref, bg_ref, mod_ref, wor_ref, wos_ref, wout_ref, o_ref):
    d = x_ref.shape[2]
    gr = _sigmoid(gate_ref[0, :, 0:d] + bg_ref[:, 0:d])
    gs = _sigmoid(gate_ref[0, :, d:2 * d] + bg_ref[:, d:2 * d])
    merged = gr * _dot(yr_ref[0], wor_ref[...]) + gs * _dot(ys_ref[0], wos_ref[...])
    o_ref[0] = x_ref[0] + mod_ref[0, 2:3, :] * _dot(merged.astype(BF16), wout_ref[...])


def _merge(x, y_rwkv, y_sb, gates, b_gate, mod, w_or, w_os, w_out, tm=512):
    bsz, s, d = x.shape
    full = lambda a: pl.BlockSpec(a.shape, lambda b, i: (0, 0))
    return pl.pallas_call(
        _merge_kernel,
        grid=(bsz, s // tm),
        in_specs=[
            pl.BlockSpec((1, tm, d), lambda b, i: (b, i, 0)),
            pl.BlockSpec((1, tm, MIX_WIDTH), lambda b, i: (b, i, 0)),
            pl.BlockSpec((1, tm, MIX_WIDTH), lambda b, i: (b, i, 0)),
            pl.BlockSpec((1, tm, 2 * d), lambda b, i: (b, i, 0)),
            pl.BlockSpec((1, 2 * d), lambda b, i: (0, 0)),
            pl.BlockSpec((1, 6, d), lambda b, i: (b, 0, 0)),
            full(w_or), full(w_os), full(w_out),
        ],
        out_specs=pl.BlockSpec((1, tm, d), lambda b, i: (b, i, 0)),
        out_shape=jax.ShapeDtypeStruct((bsz, s, d), F32),
        compiler_params=_params(("arbitrary", "arbitrary")),
        name="merge_out",
    )(x, y_rwkv, y_sb, gates, b_gate, mod, w_or, w_os, w_out)


def _ffn_kernel(x_ref, halo_ref, mod_ref, g_ref, wup_ref, cw_ref, cb_ref, wdn_ref, o_ref, act_s, *u_bufs, tf):
    tm = x_ref.shape[1]
    dff = wdn_ref.shape[0]
    x = x_ref[0]

    def norm_mod(t):
        ms = jnp.mean(t * t, axis=-1, keepdims=True)
        hh = t * lax.rsqrt(ms + RMS_EPS) * g_ref[...]
        return hh * (1.0 + mod_ref[0, 4:5, :]) + mod_ref[0, 3:4, :]

    h = norm_mod(x).astype(BF16)
    keep = (pl.program_id(1) > 0).astype(F32)
    hh = (norm_mod(halo_ref[0]) * keep).astype(BF16)
    hcat = jnp.concatenate([hh, h], axis=0)

    n_tiles = dff // tf

    def up(j):
        for half in range(2):
            col0 = half * dff + j * tf
            u_bufs[(2 * j + half) % len(u_bufs)][...] = _dot(hcat, wup_ref[:, col0:col0 + tf])

    def conv(j, half):
        cols = slice(half * dff + j * tf, half * dff + (j + 1) * tf)
        u_s = u_bufs[(2 * j + half) % len(u_bufs)]
        cw = cw_ref[:, cols]
        out = cb_ref[:, cols] + cw[2:3, :] * u_s[CONV_HALO:, :]
        out = out + cw[1:2, :] * u_s[CONV_HALO - 1:CONV_HALO - 1 + tm, :]
        out = out + cw[0:1, :] * u_s[CONV_HALO - 2:CONV_HALO - 2 + tm, :]
        return out

    def act(j):
        gt = conv(j, 1)
        act_s[:, j * tf:(j + 1) * tf] = (gt * _sigmoid(gt) * conv(j, 0)).astype(BF16)

    up(0)
    if n_tiles > 1:
        up(1)
    act(0)
    for j in range(n_tiles - 1):
        if j + 2 < n_tiles:
            up(j + 2)
        act(j + 1)
    o_ref[0] = x + mod_ref[0, 5:6, :] * _dot(act_s[...], wdn_ref[...])


def _ffn(x, mod, norm2_g, w_up, conv_w, conv_b, w_down, tm=512, tf=MXU_TILE):
    bsz, s, d = x.shape
    dff = w_down.shape[0]
    pad = -dff % tf
    halves = lambda a: jnp.concatenate(
        [jnp.pad(a[..., :dff], ((0, 0), (0, pad))), jnp.pad(a[..., dff:], ((0, 0), (0, pad)))], axis=-1)
    w_up = halves(w_up).astype(BF16)
    conv_w = halves(conv_w)
    conv_b = halves(conv_b)
    w_down = jnp.pad(w_down, ((0, pad), (0, 0))).astype(BF16)
    hb = tm // CONV_HALO
    full = lambda a: pl.BlockSpec(a.shape, lambda b, i: (0, 0), pipeline_mode=pl.Buffered(1))
    return pl.pallas_call(
        functools.partial(_ffn_kernel, tf=tf),
        grid=(bsz, s // tm),
        in_specs=[
            pl.BlockSpec((1, tm, d), lambda b, i: (b, i, 0)),
            pl.BlockSpec((1, CONV_HALO, d), lambda b, i: (b, jnp.maximum(i * hb - 1, 0), 0)),
            pl.BlockSpec((1, 6, d), lambda b, i: (b, 0, 0)),
            pl.BlockSpec((1, d), lambda b, i: (0, 0)),
            full(w_up), full(conv_w), full(conv_b), full(w_down),
        ],
        out_specs=pl.BlockSpec((1, tm, d), lambda b, i: (b, i, 0)),
        out_shape=jax.ShapeDtypeStruct((bsz, s, d), F32),
        scratch_shapes=[pltpu.VMEM((tm, dff + pad), BF16)]
        + [pltpu.VMEM((CONV_HALO + tm, tf), F32) for _ in range(4)],
        compiler_params=_params(("arbitrary", "arbitrary")),
        name="conv_ffn",
    )(x, x, mod, norm2_g, w_up, conv_w, conv_b, w_down)


def _layer(x, c, w_ada, b_ada, norm1_g, w_in, b_gate, rwkv_mu, rwkv_w0, rwkv_w2, rwkv_a0,
           rwkv_a2, rwkv_g2, rwkv_k_k, rwkv_k_a, rwkv_r_k, rwkv_lnx_g, rwkv_lnx_b, sb_q_g,
           sb_k_g, w_o_rwkv, w_o_sb, w_out, norm2_g, w_up, conv_w, conv_b, w_down):
    bsz, s, d = x.shape
    w = MIX_WIDTH
    rwkv_in = 3 * w + DECAY_LORA + ICLR_LORA + GATE_LORA
    sb_in = 3 * w
    pad = RWKV_PAD - rwkv_in

    w_all = jnp.concatenate([
        w_in[:, :rwkv_in], jnp.zeros((d, pad), F32),
        w_in[:, rwkv_in + sb_in:], w_in[:, rwkv_in:rwkv_in + sb_in]], axis=1).astype(BF16)
    mu = jnp.concatenate([rwkv_mu, jnp.zeros((pad,), F32)]).reshape(1, RWKV_PAD)
    wl = jnp.zeros((DECAY_LORA + ICLR_LORA, 2 * w), F32)
    wl = wl.at[:DECAY_LORA, :w].set(rwkv_w2).at[DECAY_LORA:, w:].set(rwkv_a2).astype(BF16)
    wg = jnp.zeros((256, w), F32).at[:GATE_LORA].set(rwkv_g2).astype(BF16)
    head_id = jnp.arange(w) // HEAD_DIM
    bd_ones = (head_id[:, None] == head_id[None, :]).astype(BF16)
    row = lambda a: a.reshape(1, -1)

    mod = _modulation(c, w_ada, b_ada).reshape(bsz, 6, d)
    f_rwkv, gates, q, k, v = _in_projection(
        x, mod, row(norm1_g), w_all, bd_ones,
        row(jnp.tile(sb_q_g, N_HEADS)), row(jnp.tile(sb_k_g, N_HEADS)))
    y_rwkv, y_sb = _token_mixers(
        f_rwkv, mu, row(rwkv_w0), row(rwkv_a0), row(rwkv_k_k), row(rwkv_k_a), row(rwkv_r_k),
        row(rwkv_lnx_g), row(rwkv_lnx_b), wl, wg, bd_ones, q, k, v)
    x = _merge(x, y_rwkv, y_sb, gates, row(b_gate), mod,
               w_o_rwkv.astype(BF16), w_o_sb.astype(BF16), w_out.astype(BF16))
    return _ffn(x, mod, row(norm2_g), w_up, conv_w, row(conv_b), w_down)


def kernel(x, c, w_ada, b_ada, norm1_g, w_in, b_gate, rwkv_mu, rwkv_w0, rwkv_w2, rwkv_a0, rwkv_a2, rwkv_g2, rwkv_k_k, rwkv_k_a, rwkv_r_k, rwkv_lnx_g, rwkv_lnx_b, sb_q_g, sb_k_g, w_o_rwkv, w_o_sb, w_out, norm2_g, w_up, conv_w, conv_b, w_down):
    params = (w_ada, b_ada, norm1_g, w_in, b_gate, rwkv_mu, rwkv_w0, rwkv_w2, rwkv_a0, rwkv_a2,
              rwkv_g2, rwkv_k_k, rwkv_k_a, rwkv_r_k, rwkv_lnx_g, rwkv_lnx_b, sb_q_g, sb_k_g,
              w_o_rwkv, w_o_sb, w_out, norm2_g, w_up, conv_w, conv_b, w_down)
    for layer in range(w_ada.shape[0]):
        x = _layer(x, c, *(p[layer] for p in params))
    return x
```

```python
import functools

import jax
import jax.numpy as jnp
from jax import lax
from jax.experimental import pallas as pl
from jax.experimental.pallas import tpu as pltpu

F32 = jnp.float32
BF16 = jnp.bfloat16

HEAD_DIM = 64
N_HEADS = 8
MIX_WIDTH = HEAD_DIM * N_HEADS
DECAY_LORA = 64
ICLR_LORA = 64
GATE_LORA = 160
RWKV_PAD = 2048
CONV_WIDTH = 3
CONV_HALO = 8
RMS_EPS = 1e-6
GN_EPS = 64e-5
L2_EPS = 1e-12
CHUNK = 64
LOG2E = 1.4426950408889634
MASKED_SCORE = -1e30
SB_UNROLL = 4
MXU_TILE = 256

VMEM_LIMIT = 56 * 1024 * 1024


def _dot(a, b):
    return jnp.dot(a, b, preferred_element_type=F32)


def _dot_nt(a, b):
    return lax.dot_general(a, b, (((1,), (1,)), ((), ())), preferred_element_type=F32)


def _bmm(a, b):
    return lax.dot_general(a, b, (((2,), (1,)), ((0,), (0,))), preferred_element_type=F32)


def _bmm_nt(a, b):
    return lax.dot_general(a, b, (((2,), (2,)), ((0,), (0,))), preferred_element_type=F32)


def _bmm_tn(a, b):
    return lax.dot_general(a, b, (((1,), (1,)), ((0,), (0,))), preferred_element_type=F32)


def _sigmoid(x):
    return 1.0 / (1.0 + jnp.exp(-x))


def _softplus(x):
    return jnp.maximum(x, 0.0) + jnp.log(1.0 + jnp.exp(-jnp.abs(x)))


def _params(sem):
    return pltpu.CompilerParams(dimension_semantics=sem, vmem_limit_bytes=VMEM_LIMIT)


def _mod_kernel(c_ref, w_ref, b_ref, o_ref):
    c = c_ref[...]
    sc = (c * _sigmoid(c)).astype(BF16)
    o_ref[...] = _dot(sc, w_ref[...].astype(BF16)) + b_ref[...]


def _modulation(c, w_ada, b_ada, tn=1536):
    bsz, d = c.shape
    n = w_ada.shape[1]
    return pl.pallas_call(
        _mod_kernel,
        grid=(n // tn,),
        in_specs=[
            pl.BlockSpec((bsz, d), lambda j: (0, 0)),
            pl.BlockSpec((d, tn), lambda j: (0, j)),
            pl.BlockSpec((1, tn), lambda j: (0, j)),
        ],
        out_specs=pl.BlockSpec((bsz, tn), lambda j: (0, j)),
        out_shape=jax.ShapeDtypeStruct((bsz, n), F32),
        compiler_params=_params(("arbitrary",)),
        name="adaln_mod",
    )(c, w_ada, b_ada.reshape(1, n))


def _inproj_kernel(x_ref, mod_ref, g_ref, w_ref, bd_ref, qg_ref, kg_ref,
                   rw_ref, gate_ref, q_ref, k_ref, v_ref):
    x = x_ref[0]
    ms = jnp.mean(x * x, axis=-1, keepdims=True)
    h = x * lax.rsqrt(ms + RMS_EPS) * g_ref[...]
    h = h * (1.0 + mod_ref[0, 1:2, :]) + mod_ref[0, 0:1, :]
    hb = h.astype(BF16)
    cw = MIX_WIDTH
    for j in range(RWKV_PAD // cw):
        rw_ref[0, :, j * cw:(j + 1) * cw] = _dot(hb, w_ref[:, j * cw:(j + 1) * cw])
    off = RWKV_PAD
    for j in range(2 * x.shape[1] // cw):
        gate_ref[0, :, j * cw:(j + 1) * cw] = _dot(hb, w_ref[:, off + j * cw:off + (j + 1) * cw])
    off += 2 * x.shape[1]

    def head_norm(t, gain):
        msq = _dot((t * t).astype(BF16), bd_ref[...]) * (1.0 / HEAD_DIM)
        return t * lax.rsqrt(msq + RMS_EPS) * gain

    q = head_norm(_dot(hb, w_ref[:, off:off + cw]), qg_ref[...]) * (HEAD_DIM ** -0.5)
    k = head_norm(_dot(hb, w_ref[:, off + cw:off + 2 * cw]), kg_ref[...])
    q_ref[0] = q.astype(BF16)
    k_ref[0] = k.astype(BF16)
    v_ref[0] = _dot(hb, w_ref[:, off + 2 * cw:off + 3 * cw]).astype(BF16)


def _in_projection(x, mod, norm1_g, w_all, bd_ones, qg, kg, tm=512):
    bsz, s, d = x.shape
    n_all = w_all.shape[1]
    head_shape = jax.ShapeDtypeStruct((bsz, s, MIX_WIDTH), BF16)
    head_spec = pl.BlockSpec((1, tm, MIX_WIDTH), lambda b, i: (b, i, 0))
    return pl.pallas_call(
        _inproj_kernel,
        grid=(bsz, s // tm),
        in_specs=[
            pl.BlockSpec((1, tm, d), lambda b, i: (b, i, 0)),
            pl.BlockSpec((1, 6, d), lambda b, i: (b, 0, 0)),
            pl.BlockSpec((1, d), lambda b, i: (0, 0)),
            pl.BlockSpec((d, n_all), lambda b, i: (0, 0), pipeline_mode=pl.Buffered(1)),
            pl.BlockSpec((MIX_WIDTH, MIX_WIDTH), lambda b, i: (0, 0)),
            pl.BlockSpec((1, MIX_WIDTH), lambda b, i: (0, 0)),
            pl.BlockSpec((1, MIX_WIDTH), lambda b, i: (0, 0)),
        ],
        out_specs=[
            pl.BlockSpec((1, tm, RWKV_PAD), lambda b, i: (b, i, 0)),
            pl.BlockSpec((1, tm, 2 * d), lambda b, i: (b, i, 0)),
            head_spec, head_spec, head_spec,
        ],
        out_shape=[
            jax.ShapeDtypeStruct((bsz, s, RWKV_PAD), F32),
            jax.ShapeDtypeStruct((bsz, s, 2 * d), F32),
            head_shape, head_shape, head_shape,
        ],
        compiler_params=_params(("arbitrary", "arbitrary")),
        name="in_proj",
    )(x, mod, norm1_g, w_all, bd_ones, qg, kg)


def _rwkv_kernel(f_ref, mu_ref, w0_ref, a0_ref, kkw_ref, kaw_ref, rk_ref, lng_ref, lnb_ref,
                 wl_ref, wg_ref, bd_ref, cum_ref,
                 y_ref,
                 prev_s, h_s, at_s, rt_s, bt_s, kt_s, bh_s, kh_s, v_s, gc_s, yn_s):
    ts = f_ref.shape[1]
    n_chunks = ts // CHUNK
    w = MIX_WIDTH

    @pl.when(pl.program_id(1) == 0)
    def _():
        prev_s[...] = jnp.zeros_like(prev_s)
        h_s[...] = jnp.zeros_like(h_s)

    f = f_ref[0]
    row = lax.broadcasted_iota(jnp.int32, f.shape, 0)
    shifted = jnp.where(row == 0, prev_s[...], pltpu.roll(f, 1, axis=0))
    prev_s[...] = f[ts - 1:ts, :]
    f = f + (shifted - f) * mu_ref[...]

    r = f[:, 0:w]
    k = f[:, w:2 * w]
    v = f[:, 2 * w:3 * w]
    lora_in = f[:, 3 * w:3 * w + 128]
    gate_in = f[:, 3 * w + 128:3 * w + 384]
    lane = lax.broadcasted_iota(jnp.int32, lora_in.shape, 1)
    lora_act = jnp.where(lane < DECAY_LORA, jnp.tanh(lora_in), lora_in)
    lora = _dot(lora_act.astype(BF16), wl_ref[...])
    g = _dot(_sigmoid(gate_in).astype(BF16), wg_ref[...])

    w_log = -_softplus(-(w0_ref[...] + lora[:, 0:w])) - 0.5
    ld = -jnp.exp(w_log)
    a = _sigmoid(a0_ref[...] + lora[:, w:2 * w])
    kk = k * kkw_ref[...]
    ss = _dot((kk * kk).astype(BF16), bd_ref[...])
    kkn = kk * lax.rsqrt(jnp.maximum(ss, L2_EPS * L2_EPS))
    k2 = k * (1.0 + (a - 1.0) * kaw_ref[...])
    b = kkn * a
    bonus = _dot((r * k2 * rk_ref[...]).astype(BF16), bd_ref[...]) * v

    ld_hi = ld.astype(BF16)
    ld_lo = (ld - ld_hi.astype(F32)).astype(BF16)
    cl = _dot(cum_ref[...], ld_hi) + _dot(cum_ref[...], ld_lo)
    cle = jnp.concatenate([jnp.broadcast_to(cl[(c + 1) * CHUNK - 1:(c + 1) * CHUNK, :], (CHUNK, w))
                           for c in range(n_chunks)], axis=0)
    e_neg = jnp.exp(-cl)
    e_end = jnp.exp(cle - cl)
    at_s[...] = (-kkn * jnp.exp(cl - ld)).astype(BF16)
    rt_s[...] = (r * jnp.exp(cl)).astype(BF16)
    bt_s[...] = (b * e_neg).astype(BF16)
    kt_s[...] = (k2 * e_neg).astype(BF16)
    bh_s[...] = (b * e_end).astype(BF16)
    kh_s[...] = (k2 * e_end).astype(BF16)
    v_s[...] = v.astype(BF16)
    gc_s[...] = jnp.exp(cle)

    ri = lax.broadcasted_iota(jnp.int32, (CHUNK, CHUNK), 0)
    ci = lax.broadcasted_iota(jnp.int32, (CHUNK, CHUNK), 1)
    strict = ri > ci
    incl = ri >= ci
    eye = ri == ci

    def blocks(ref):
        return jnp.stack([ref[c * CHUNK:(c + 1) * CHUNK, hd * HEAD_DIM:(hd + 1) * HEAD_DIM]
                          for c in range(n_chunks) for hd in range(N_HEADS)])

    at, rt, bt, kt, bh, kh, vv = (blocks(s) for s in (at_s, rt_s, bt_s, kt_s, bh_s, kh_s, v_s))
    a_ab = jnp.where(strict, _bmm_nt(at, bt), 0.0)
    a_ak = jnp.where(strict, _bmm_nt(at, kt), 0.0).astype(BF16)
    a_rb = jnp.where(incl, _bmm_nt(rt, bt), 0.0).astype(BF16)
    a_rk = jnp.where(incl, _bmm_nt(rt, kt), 0.0).astype(BF16)
    t = jnp.where(eye, 1.0, a_ab)
    p = a_ab
    for _ in range(5):
        pb = p.astype(BF16)
        p = _bmm(pb, pb)
        t = t + _bmm(t.astype(BF16), p.astype(BF16))
    tb = t.astype(BF16)
    p1 = _bmm(tb, at).astype(BF16)
    p2 = _bmm(tb, _bmm(a_ak, vv).astype(BF16)).astype(BF16)
    q1 = (rt.astype(F32) + _bmm(a_rb, p1)).astype(BF16)
    q2 = _bmm(a_rb, p2) + _bmm(a_rk, vv)
    gc = jnp.stack([gc_s[c * CHUNK:c * CHUNK + 1, hd * HEAD_DIM:(hd + 1) * HEAD_DIM]
                    for c in range(n_chunks) for hd in range(N_HEADS)])
    m = (jnp.where(eye, gc, 0.0) + _bmm_tn(bh, p1)).astype(BF16)
    n = _bmm_tn(jnp.concatenate([bh, kh], axis=1), jnp.concatenate([p2, vv], axis=1))

    h = h_s[...]
    for c in range(n_chunks):
        sl = slice(c * N_HEADS, (c + 1) * N_HEADS)
        hb = h.astype(BF16)
        y = _bmm(q1[sl], hb) + q2[sl]
        h = _bmm(m[sl], hb) + n[sl]
        mean = jnp.mean(y, axis=-1, keepdims=True)
        yc = y - mean
        var = jnp.mean(yc * yc, axis=-1, keepdims=True)
        yn = yc * lax.rsqrt(var + GN_EPS)
        for hd in range(N_HEADS):
            yn_s[c * CHUNK:(c + 1) * CHUNK, hd * HEAD_DIM:(hd + 1) * HEAD_DIM] = yn[hd]
    h_s[...] = h

    y = (yn_s[...] * lng_ref[...] + lnb_ref[...] + bonus) * g
    y_ref[0] = y.astype(BF16)


def _sb_kernel(qi, q_ref, k_ref, v_ref, suffix_ref, o_ref, z_s, sp_s, acc_s, carry_s):
    tq = q_ref.shape[1]
    tk = suffix_ref.shape[0]
    ratio = tq // tk
    suffix = suffix_ref[...]
    lane = lax.broadcasted_iota(jnp.int32, (tq, 2 * HEAD_DIM), 1)
    qp = q_ref[0]
    zero = jnp.zeros_like(qp)
    q_heads = (jnp.where(lane < HEAD_DIM, qp, zero), jnp.where(lane >= HEAD_DIM, qp, zero))

    def key_block(n):
        j = jnp.maximum(qi * ratio + (ratio - 1) - n, 0)
        return j, pl.ds(pl.multiple_of(j * tk, tk), tk)

    def first_row(n):
        return (ratio - 1 - n) * tk if is_own(n) else 0

    def is_own(n):
        return isinstance(n, int) and n < ratio

    def scores(n, slot):
        j, rows = key_block(n)
        r0 = first_row(n)
        kb = k_ref[0, rows, :]
        for hd in range(2):
            qh = q_heads[hd][r0:, :]
            half = 0 if is_own(n) else qh.shape[0] // 2
            if half:
                z = jnp.concatenate([_dot_nt(qh[:half], kb), _dot_nt(qh[half:], kb)], axis=0)
            else:
                z = _dot_nt(qh, kb)
            zb = z.astype(BF16)
            sp = jnp.maximum(zb, 0.0) + jnp.log(1.0 + jnp.exp2(jnp.abs(zb) * (-LOG2E)))
            if is_own(n):
                ri = lax.broadcasted_iota(jnp.int32, z.shape, 0)
                ci = lax.broadcasted_iota(jnp.int32, z.shape, 1)
                causal = ci < ri
                sp = jnp.where(causal, sp, jnp.zeros_like(sp))
                z = jnp.where(causal, z, MASKED_SCORE)
            z_s[hd, slot, r0:, :] = z
            sp_s[hd, slot, r0:, :] = sp

    def accumulate(n, slot):
        _, rows = key_block(n)
        r0 = first_row(n)
        vb = v_ref[0, rows, :]
        for hd in range(2):
            spv = sp_s[hd, slot, r0:, :]
            half = 0 if is_own(n) else spv.shape[0] // 2
            if half:
                cs = jnp.concatenate([_dot(spv[:half], suffix), _dot(spv[half:], suffix)], axis=0)
            else:
                cs = _dot(spv, suffix)
            carry = carry_s[hd, r0:, :]
            a = jnp.exp(z_s[hd, slot, r0:, :] - cs - jnp.concatenate([carry] * (tk // 128), axis=1))
            acc_s[hd, r0:, :] += _dot(a.astype(BF16), vb)
            carry_s[hd, r0:, :] = carry + jnp.broadcast_to(cs[:, 0:1], carry.shape)

    acc_s[...] = jnp.zeros_like(acc_s)
    carry_s[...] = jnp.zeros_like(carry_s)
    scores(0, 0)
    for n in range(ratio):
        scores(n + 1, (n + 1) % 2)
        accumulate(n, n % 2)

    def group(p, _):
        n0 = ratio + SB_UNROLL * p
        for u in range(SB_UNROLL):
            scores(n0 + u + 1, (u + 1) % 2)
            accumulate(n0 + u, u % 2)
        return 0

    lax.fori_loop(0, qi * (ratio // SB_UNROLL), group, 0)
    o_ref[0] = jnp.where(lane < HEAD_DIM, acc_s[0], acc_s[1]).astype(BF16)


N_RWKV_IN = 13
N_RWKV_SCRATCH = 11


def _mix_kernel(*refs):
    rw_in, sb_in = refs[:N_RWKV_IN], refs[N_RWKV_IN:N_RWKV_IN + 4]
    y_ref, o_ref = refs[N_RWKV_IN + 4:N_RWKV_IN + 6]
    scratch = refs[N_RWKV_IN + 6:]
    n_q = sb_in[1].shape[1] // sb_in[0].shape[1]
    _rwkv_kernel(*rw_in, y_ref, *scratch[:N_RWKV_SCRATCH])
    _sb_kernel(pl.program_id(1) % n_q, *sb_in, o_ref, *scratch[N_RWKV_SCRATCH:])


def _token_mixers(f_rwkv, mu, w0, a0, k_k, k_a, r_k, lnx_g, lnx_b, wl, wg, bd_ones, q, k, v,
                  ts=256, tq=1024, tk=256):
    bsz, s, fw = f_rwkv.shape
    w = MIX_WIDTH
    pw = 2 * HEAD_DIM
    n_q = s // tq
    assert s // ts == (w // pw) * n_q
    assert (tq // tk) % SB_UNROLL == 0 and SB_UNROLL % 2 == 0
    ri = jnp.arange(ts)[:, None]
    ci = jnp.arange(ts)[None, :]
    same = (ri // CHUNK) == (ci // CHUNK)
    cum = (same & (ci <= ri)).astype(BF16)
    suffix = (jnp.arange(tk)[:, None] >= jnp.arange(tk)[None, :]).astype(BF16)
    vec = lambda n: pl.BlockSpec((1, n), lambda b, i: (0, 0))
    full = lambda a: pl.BlockSpec(a.shape, lambda b, i: (0, 0))
    return pl.pallas_call(
        _mix_kernel,
        grid=(bsz, s // ts),
        in_specs=[
            pl.BlockSpec((1, ts, fw), lambda b, i: (b, i, 0)),
            vec(fw), vec(w), vec(w), vec(w), vec(w), vec(w), vec(w), vec(w),
            full(wl), full(wg), full(bd_ones), full(cum),
            pl.BlockSpec((1, tq, pw), lambda b, i: (b, i % n_q, i // n_q)),
            pl.BlockSpec((1, s, pw), lambda b, i: (b, 0, i // n_q)),
            pl.BlockSpec((1, s, pw), lambda b, i: (b, 0, i // n_q)),
            pl.BlockSpec((tk, tk), lambda b, i: (0, 0)),
        ],
        out_specs=[
            pl.BlockSpec((1, ts, w), lambda b, i: (b, i, 0)),
            pl.BlockSpec((1, tq, pw), lambda b, i: (b, i % n_q, i // n_q)),
        ],
        out_shape=[jax.ShapeDtypeStruct((bsz, s, w), BF16), jax.ShapeDtypeStruct((bsz, s, w), BF16)],
        scratch_shapes=[
            pltpu.VMEM((1, fw), F32),
            pltpu.VMEM((N_HEADS, HEAD_DIM, HEAD_DIM), F32),
            pltpu.VMEM((ts, w), BF16), pltpu.VMEM((ts, w), BF16), pltpu.VMEM((ts, w), BF16),
            pltpu.VMEM((ts, w), BF16), pltpu.VMEM((ts, w), BF16), pltpu.VMEM((ts, w), BF16),
            pltpu.VMEM((ts, w), BF16),
            pltpu.VMEM((ts, w), F32),
            pltpu.VMEM((ts, w), F32),
            pltpu.VMEM((2, 2, tq, tk), F32),
            pltpu.VMEM((2, 2, tq, tk), BF16),
            pltpu.VMEM((2, tq, pw), F32),
            pltpu.VMEM((2, tq, pw), F32),
        ],
        compiler_params=_params(("arbitrary", "arbitrary")),
        name="token_mixers",
    )(f_rwkv, mu, w0, a0, k_k, k_a, r_k, lnx_g, lnx_b, wl, wg, bd_ones, cum, q, k, v, suffix)


def _merge_kernel(x_ref, yr_ref, ys_ref, gate_ref, bg_ref, mod_ref, wor_ref, wos_ref, wout_ref, o_ref):
    d = x_ref.shape[2]
    gr = _sigmoid(gate_ref[0, :, 0:d] + bg_ref[:, 0:d])
    gs = _sigmoid(gate_ref[0, :, d:2 * d] + bg_ref[:, d:2 * d])
    merged = gr * _dot(yr_ref[0], wor_ref[...]) + gs * _dot(ys_ref[0], wos_ref[...])
    o_ref[0] = x_ref[0] + mod_ref[0, 2:3, :] * _dot(merged.astype(BF16), wout_ref[...])


def _merge(x, y_rwkv, y_sb, gates, b_gate, mod, w_or, w_os, w_out, tm=512):
    bsz, s, d = x.shape
    full = lambda a: pl.BlockSpec(a.shape, lambda b, i: (0, 0))
    return pl.pallas_call(
        _merge_kernel,
        grid=(bsz, s // tm),
        in_specs=[
            pl.BlockSpec((1, tm, d), lambda b, i: (b, i, 0)),
            pl.BlockSpec((1, tm, MIX_WIDTH), lambda b, i: (b, i, 0)),
            pl.BlockSpec((1, tm, MIX_WIDTH), lambda b, i: (b, i, 0)),
            pl.BlockSpec((1, tm, 2 * d), lambda b, i: (b, i, 0)),
            pl.BlockSpec((1, 2 * d), lambda b, i: (0, 0)),
            pl.BlockSpec((1, 6, d), lambda b, i: (b, 0, 0)),
            full(w_or), full(w_os), full(w_out),
        ],
        out_specs=pl.BlockSpec((1, tm, d), lambda b, i: (b, i, 0)),
        out_shape=jax.ShapeDtypeStruct((bsz, s, d), F32),
        compiler_params=_params(("arbitrary", "arbitrary")),
        name="merge_out",
    )(x, y_rwkv, y_sb, gates, b_gate, mod, w_or, w_os, w_out)


def _ffn_kernel(x_ref, halo_ref, mod_ref, g_ref, wup_ref, cw_ref, cb_ref, wdn_ref, o_ref, act_s, *u_bufs, tf):
    tm = x_ref.shape[1]
    dff = wdn_ref.shape[0]
    x = x_ref[0]

    def norm_mod(t):
        ms = jnp.mean(t * t, axis=-1, keepdims=True)
        hh = t * lax.rsqrt(ms + RMS_EPS) * g_ref[...]
        return hh * (1.0 + mod_ref[0, 4:5, :]) + mod_ref[0, 3:4, :]

    h = norm_mod(x).astype(BF16)
    keep = (pl.program_id(1) > 0).astype(F32)
    hh = (norm_mod(halo_ref[0]) * keep).astype(BF16)
    hcat = jnp.concatenate([hh, h], axis=0)

    n_tiles = dff // tf

    def up(j):
        for half in range(2):
            col0 = half * dff + j * tf
            u_bufs[(2 * j + half) % len(u_bufs)][...] = _dot(hcat, wup_ref[:, col0:col0 + tf])

    def conv(j, half):
        cols = slice(half * dff + j * tf, half * dff + (j + 1) * tf)
        u_s = u_bufs[(2 * j + half) % len(u_bufs)]
        cw = cw_ref[:, cols]
        out = cb_ref[:, cols] + cw[2:3, :] * u_s[CONV_HALO:, :]
        out = out + cw[1:2, :] * u_s[CONV_HALO - 1:CONV_HALO - 1 + tm, :]
        out = out + cw[0:1, :] * u_s[CONV_HALO - 2:CONV_HALO - 2 + tm, :]
        return out

    def act(j):
        gt = conv(j, 1)
        act_s[:, j * tf:(j + 1) * tf] = (gt * _sigmoid(gt) * conv(j, 0)).astype(BF16)

    up(0)
    if n_tiles > 1:
        up(1)
    act(0)
    for j in range(n_tiles - 1):
        if j + 2 < n_tiles:
            up(j + 2)
        act(j + 1)
    o_ref[0] = x + mod_ref[0, 5:6, :] * _dot(act_s[...], wdn_ref[...])


def _ffn(x, mod, norm2_g, w_up, conv_w, conv_b, w_down, tm=512, tf=MXU_TILE):
    bsz, s, d = x.shape
    dff = w_down.shape[0]
    pad = -dff % tf
    halves = lambda a: jnp.concatenate(
        [jnp.pad(a[..., :dff], ((0, 0), (0, pad))), jnp.pad(a[..., dff:], ((0, 0), (0, pad)))], axis=-1)
    w_up = halves(w_up).astype(BF16)
    conv_w = halves(conv_w)
    conv_b = halves(conv_b)
    w_down = jnp.pad(w_down, ((0, pad), (0, 0))).astype(BF16)
    hb = tm // CONV_HALO
    full = lambda a: pl.BlockSpec(a.shape, lambda b, i: (0, 0), pipeline_mode=pl.Buffered(1))
    return pl.pallas_call(
        functools.partial(_ffn_kernel, tf=tf),
        grid=(bsz, s // tm),
        in_specs=[
            pl.BlockSpec((1, tm, d), lambda b, i: (b, i, 0)),
            pl.BlockSpec((1, CONV_HALO, d), lambda b, i: (b, jnp.maximum(i * hb - 1, 0), 0)),
            pl.BlockSpec((1, 6, d), lambda b, i: (b, 0, 0)),
            pl.BlockSpec((1, d), lambda b, i: (0, 0)),
            full(w_up), full(conv_w), full(conv_b), full(w_down),
        ],
        out_specs=pl.BlockSpec((1, tm, d), lambda b, i: (b, i, 0)),
        out_shape=jax.ShapeDtypeStruct((bsz, s, d), F32),
        scratch_shapes=[pltpu.VMEM((tm, dff + pad), BF16)]
        + [pltpu.VMEM((CONV_HALO + tm, tf), F32) for _ in range(4)],
        compiler_params=_params(("arbitrary", "arbitrary")),
        name="conv_ffn",
    )(x, x, mod, norm2_g, w_up, conv_w, conv_b, w_down)


def _layer(x, c, w_ada, b_ada, norm1_g, w_in, b_gate, rwkv_mu, rwkv_w0, rwkv_w2, rwkv_a0,
           rwkv_a2, rwkv_g2, rwkv_k_k, rwkv_k_a, rwkv_r_k, rwkv_lnx_g, rwkv_lnx_b, sb_q_g,
           sb_k_g, w_o_rwkv, w_o_sb, w_out, norm2_g, w_up, conv_w, conv_b, w_down):
    bsz, s, d = x.shape
    w = MIX_WIDTH
    rwkv_in = 3 * w + DECAY_LORA + ICLR_LORA + GATE_LORA
    sb_in = 3 * w
    pad = RWKV_PAD - rwkv_in

    w_all = jnp.concatenate([
        w_in[:, :rwkv_in], jnp.zeros((d, pad), F32),
        w_in[:, rwkv_in + sb_in:], w_in[:, rwkv_in:rwkv_in + sb_in]], axis=1).astype(BF16)
    mu = jnp.concatenate([rwkv_mu, jnp.zeros((pad,), F32)]).reshape(1, RWKV_PAD)
    wl = jnp.zeros((DECAY_LORA + ICLR_LORA, 2 * w), F32)
    wl = wl.at[:DECAY_LORA, :w].set(rwkv_w2).at[DECAY_LORA:, w:].set(rwkv_a2).astype(BF16)
    wg = jnp.zeros((256, w), F32).at[:GATE_LORA].set(rwkv_g2).astype(BF16)
    head_id = jnp.arange(w) // HEAD_DIM
    bd_ones = (head_id[:, None] == head_id[None, :]).astype(BF16)
    row = lambda a: a.reshape(1, -1)

    mod = _modulation(c, w_ada, b_ada).reshape(bsz, 6, d)
    f_rwkv, gates, q, k, v = _in_projection(
        x, mod, row(norm1_g), w_all, bd_ones,
        row(jnp.tile(sb_q_g, N_HEADS)), row(jnp.tile(sb_k_g, N_HEADS)))
    y_rwkv, y_sb = _token_mixers(
        f_rwkv, mu, row(rwkv_w0), row(rwkv_a0), row(rwkv_k_k), row(rwkv_k_a), row(rwkv_r_k),
        row(rwkv_lnx_g), row(rwkv_lnx_b), wl, wg, bd_ones, q, k, v)
    x = _merge(x, y_rwkv, y_sb, gates, row(b_gate), mod,
               w_o_rwkv.astype(BF16), w_o_sb.astype(BF16), w_out.astype(BF16))
    return _ffn(x, mod, row(norm2_g), w_up, conv_w, row(conv_b), w_down)


def kernel(x, c, w_ada, b_ada, norm1_g, w_in, b_gate, rwkv_mu, rwkv_w0, rwkv_w2, rwkv_a0, rwkv_a2, rwkv_g2, rwkv_k_k, rwkv_k_a, rwkv_r_k, rwkv_lnx_g, rwkv_lnx_b, sb_q_g, sb_k_g, w_o_rwkv, w_o_sb, w_out, norm2_g, w_up, conv_w, conv_b, w_down):
    params = (w_ada, b_ada, norm1_g, w_in, b_gate, rwkv_mu, rwkv_w0, rwkv_w2, rwkv_a0, rwkv_a2,
              rwkv_g2, rwkv_k_k, rwkv_k_a, rwkv_r_k, rwkv_lnx_g, rwkv_lnx_b, sb_q_g, sb_k_g,
              w_o_rwkv, w_o_sb, w_out, norm2_g, w_up, conv_w, conv_b, w_down)
    for layer in range(w_ada.shape[0]):
        x = _layer(x, c, *(p[layer] for p in params))
    return x
```

```python
import functools

import jax
import jax.numpy as jnp
from jax import lax
from jax.experimental import pallas as pl
from jax.experimental.pallas import tpu as pltpu

F32 = jnp.float32
BF16 = jnp.bfloat16

HEAD_DIM = 64
N_HEADS = 8
MIX_WIDTH = HEAD_DIM * N_HEADS
DECAY_LORA = 64
ICLR_LORA = 64
GATE_LORA = 160
RWKV_PAD = 2048
CONV_WIDTH = 3
CONV_HALO = 8
RMS_EPS = 1e-6
GN_EPS = 64e-5
L2_EPS = 1e-12
CHUNK = 64
LOG2E = 1.4426950408889634
MASKED_SCORE = -1e30
SB_UNROLL = 4
MXU_TILE = 256

VMEM_LIMIT = 56 * 1024 * 1024


def _dot(a, b):
    return jnp.dot(a, b, preferred_element_type=F32)


def _dot_nt(a, b):
    return lax.dot_general(a, b, (((1,), (1,)), ((), ())), preferred_element_type=F32)


def _bmm(a, b):
    return lax.dot_general(a, b, (((2,), (1,)), ((0,), (0,))), preferred_element_type=F32)


def _bmm_nt(a, b):
    return lax.dot_general(a, b, (((2,), (2,)), ((0,), (0,))), preferred_element_type=F32)


def _bmm_tn(a, b):
    return lax.dot_general(a, b, (((1,), (1,)), ((0,), (0,))), preferred_element_type=F32)


def _sigmoid(x):
    return 1.0 / (1.0 + jnp.exp(-x))


def _softplus(x):
    return jnp.maximum(x, 0.0) + jnp.log(1.0 + jnp.exp(-jnp.abs(x)))


def _params(sem):
    return pltpu.CompilerParams(dimension_semantics=sem, vmem_limit_bytes=VMEM_LIMIT)


def _mod_kernel(c_ref, w_ref, b_ref, o_ref):
    c = c_ref[...]
    sc = (c * _sigmoid(c)).astype(BF16)
    o_ref[...] = _dot(sc, w_ref[...].astype(BF16)) + b_ref[...]


def _modulation(c, w_ada, b_ada, tn=1536):
    bsz, d = c.shape
    n = w_ada.shape[1]
    return pl.pallas_call(
        _mod_kernel,
        grid=(n // tn,),
        in_specs=[
            pl.BlockSpec((bsz, d), lambda j: (0, 0)),
            pl.BlockSpec((d, tn), lambda j: (0, j)),
            pl.BlockSpec((1, tn), lambda j: (0, j)),
        ],
        out_specs=pl.BlockSpec((bsz, tn), lambda j: (0, j)),
        out_shape=jax.ShapeDtypeStruct((bsz, n), F32),
        compiler_params=_params(("arbitrary",)),
        name="adaln_mod",
    )(c, w_ada, b_ada.reshape(1, n))


def _inproj_kernel(x_ref, mod_ref, g_ref, w_ref, bd_ref, qg_ref, kg_ref,
                   rw_ref, gate_ref, q_ref, k_ref, v_ref):
    x = x_ref[0]
    ms = jnp.mean(x * x, axis=-1, keepdims=True)
    h = x * lax.rsqrt(ms + RMS_EPS) * g_ref[...]
    h = h * (1.0 + mod_ref[0, 1:2, :]) + mod_ref[0, 0:1, :]
    hb = h.astype(BF16)
    cw = MIX_WIDTH
    for j in range(RWKV_PAD // cw):
        rw_ref[0, :, j * cw:(j + 1) * cw] = _dot(hb, w_ref[:, j * cw:(j + 1) * cw])
    off = RWKV_PAD
    for j in range(2 * x.shape[1] // cw):
        gate_ref[0, :, j * cw:(j + 1) * cw] = _dot(hb, w_ref[:, off + j * cw:off + (j + 1) * cw])
    off += 2 * x.shape[1]

    def head_norm(t, gain):
        msq = _dot((t * t).astype(BF16), bd_ref[...]) * (1.0 / HEAD_DIM)
        return t * lax.rsqrt(msq + RMS_EPS) * gain

    q = head_norm(_dot(hb, w_ref[:, off:off + cw]), qg_ref[...]) * (HEAD_DIM ** -0.5)
    k = head_norm(_dot(hb, w_ref[:, off + cw:off + 2 * cw]), kg_ref[...])
    q_ref[0] = q.astype(BF16)
    k_ref[0] = k.astype(BF16)
    v_ref[0] = _dot(hb, w_ref[:, off + 2 * cw:off + 3 * cw]).astype(BF16)


def _in_projection(x, mod, norm1_g, w_all, bd_ones, qg, kg, tm=512):
    bsz, s, d = x.shape
    n_all = w_all.shape[1]
    head_shape = jax.ShapeDtypeStruct((bsz, s, MIX_WIDTH), BF16)
    head_spec = pl.BlockSpec((1, tm, MIX_WIDTH), lambda b, i: (b, i, 0))
    return pl.pallas_call(
        _inproj_kernel,
        grid=(bsz, s // tm),
        in_specs=[
            pl.BlockSpec((1, tm, d), lambda b, i: (b, i, 0)),
            pl.BlockSpec((1, 6, d), lambda b, i: (b, 0, 0)),
            pl.BlockSpec((1, d), lambda b, i: (0, 0)),
            pl.BlockSpec((d, n_all), lambda b, i: (0, 0), pipeline_mode=pl.Buffered(1)),
            pl.BlockSpec((MIX_WIDTH, MIX_WIDTH), lambda b, i: (0, 0)),
            pl.BlockSpec((1, MIX_WIDTH), lambda b, i: (0, 0)),
            pl.BlockSpec((1, MIX_WIDTH), lambda b, i: (0, 0)),
        ],
        out_specs=[
            pl.BlockSpec((1, tm, RWKV_PAD), lambda b, i: (b, i, 0)),
            pl.BlockSpec((1, tm, 2 * d), lambda b, i: (b, i, 0)),
            head_spec, head_spec, head_spec,
        ],
        out_shape=[
            jax.ShapeDtypeStruct((bsz, s, RWKV_PAD), F32),
            jax.ShapeDtypeStruct((bsz, s, 2 * d), F32),
            head_shape, head_shape, head_shape,
        ],
        compiler_params=_params(("arbitrary", "arbitrary")),
        name="in_proj",
    )(x, mod, norm1_g, w_all, bd_ones, qg, kg)


def _rwkv_kernel(f_ref, mu_ref, w0_ref, a0_ref, kkw_ref, kaw_ref, rk_ref, lng_ref, lnb_ref,
                 wl_ref, wg_ref, bd_ref, cum_ref,
                 y_ref,
                 prev_s, h_s, at_s, rt_s, bt_s, kt_s, bh_s, kh_s, v_s, gc_s, yn_s):
    ts = f_ref.shape[1]
    n_chunks = ts // CHUNK
    w = MIX_WIDTH

    @pl.when(pl.program_id(1) == 0)
    def _():
        prev_s[...] = jnp.zeros_like(prev_s)
        h_s[...] = jnp.zeros_like(h_s)

    f = f_ref[0]
    row = lax.broadcasted_iota(jnp.int32, f.shape, 0)
    shifted = jnp.where(row == 0, prev_s[...], pltpu.roll(f, 1, axis=0))
    prev_s[...] = f[ts - 1:ts, :]
    f = f + (shifted - f) * mu_ref[...]

    r = f[:, 0:w]
    k = f[:, w:2 * w]
    v = f[:, 2 * w:3 * w]
    lora_in = f[:, 3 * w:3 * w + 128]
    gate_in = f[:, 3 * w + 128:3 * w + 384]
    lane = lax.broadcasted_iota(jnp.int32, lora_in.shape, 1)
    lora_act = jnp.where(lane < DECAY_LORA, jnp.tanh(lora_in), lora_in)
    lora = _dot(lora_act.astype(BF16), wl_ref[...])
    g = _dot(_sigmoid(gate_in).astype(BF16), wg_ref[...])

    w_log = -_softplus(-(w0_ref[...] + lora[:, 0:w])) - 0.5
    ld = -jnp.exp(w_log)
    a = _sigmoid(a0_ref[...] + lora[:, w:2 * w])
    kk = k * kkw_ref[...]
    ss = _dot((kk * kk).astype(BF16), bd_ref[...])
    kkn = kk * lax.rsqrt(jnp.maximum(ss, L2_EPS * L2_EPS))
    k2 = k * (1.0 + (a - 1.0) * kaw_ref[...])
    b = kkn * a
    bonus = _dot((r * k2 * rk_ref[...]).astype(BF16), bd_ref[...]) * v

    ld_hi = ld.astype(BF16)
    ld_lo = (ld - ld_hi.astype(F32)).astype(BF16)
    cl = _dot(cum_ref[...], ld_hi) + _dot(cum_ref[...], ld_lo)
    cle = jnp.concatenate([jnp.broadcast_to(cl[(c + 1) * CHUNK - 1:(c + 1) * CHUNK, :], (CHUNK, w))
                           for c in range(n_chunks)], axis=0)
    e_neg = jnp.exp(-cl)
    e_end = jnp.exp(cle - cl)
    at_s[...] = (-kkn * jnp.exp(cl - ld)).astype(BF16)
    rt_s[...] = (r * jnp.exp(cl)).astype(BF16)
    bt_s[...] = (b * e_neg).astype(BF16)
    kt_s[...] = (k2 * e_neg).astype(BF16)
    bh_s[...] = (b * e_end).astype(BF16)
    kh_s[...] = (k2 * e_end).astype(BF16)
    v_s[...] = v.astype(BF16)
    gc_s[...] = jnp.exp(cle)

    ri = lax.broadcasted_iota(jnp.int32, (CHUNK, CHUNK), 0)
    ci = lax.broadcasted_iota(jnp.int32, (CHUNK, CHUNK), 1)
    strict = ri > ci
    incl = ri >= ci
    eye = ri == ci

    def blocks(ref):
        return jnp.stack([ref[c * CHUNK:(c + 1) * CHUNK, hd * HEAD_DIM:(hd + 1) * HEAD_DIM]
                          for c in range(n_chunks) for hd in range(N_HEADS)])

    at, rt, bt, kt, bh, kh, vv = (blocks(s) for s in (at_s, rt_s, bt_s, kt_s, bh_s, kh_s, v_s))
    a_ab = jnp.where(strict, _bmm_nt(at, bt), 0.0)
    a_ak = jnp.where(strict, _bmm_nt(at, kt), 0.0).astype(BF16)
    a_rb = jnp.where(incl, _bmm_nt(rt, bt), 0.0).astype(BF16)
    a_rk = jnp.where(incl, _bmm_nt(rt, kt), 0.0).astype(BF16)
    t = jnp.where(eye, 1.0, a_ab)
    p = a_ab
    for _ in range(5):
        pb = p.astype(BF16)
        p = _bmm(pb, pb)
        t = t + _bmm(t.astype(BF16), p.astype(BF16))
    tb = t.astype(BF16)
    p1 = _bmm(tb, at).astype(BF16)
    p2 = _bmm(tb, _bmm(a_ak, vv).astype(BF16)).astype(BF16)
    q1 = (rt.astype(F32) + _bmm(a_rb, p1)).astype(BF16)
    q2 = _bmm(a_rb, p2) + _bmm(a_rk, vv)
    gc = jnp.stack([gc_s[c * CHUNK:c * CHUNK + 1, hd * HEAD_DIM:(hd + 1) * HEAD_DIM]
                    for c in range(n_chunks) for hd in range(N_HEADS)])
    m = (jnp.where(eye, gc, 0.0) + _bmm_tn(bh, p1)).astype(BF16)
    n = _bmm_tn(jnp.concatenate([bh, kh], axis=1), jnp.concatenate([p2, vv], axis=1))

    h = h_s[...]
    for c in range(n_chunks):
        sl = slice(c * N_HEADS, (c + 1) * N_HEADS)
        hb = h.astype(BF16)
        y = _bmm(q1[sl], hb) + q2[sl]
        h = _bmm(m[sl], hb) + n[sl]
        mean = jnp.mean(y, axis=-1, keepdims=True)
        yc = y - mean
        var = jnp.mean(yc * yc, axis=-1, keepdims=True)
        yn = yc * lax.rsqrt(var + GN_EPS)
        for hd in range(N_HEADS):
            yn_s[c * CHUNK:(c + 1) * CHUNK, hd * HEAD_DIM:(hd + 1) * HEAD_DIM] = yn[hd]
    h_s[...] = h

    y = (yn_s[...] * lng_ref[...] + lnb_ref[...] + bonus) * g
    y_ref[0] = y.astype(BF16)


def _sb_kernel(qi, q_ref, k_ref, v_ref, suffix_ref, o_ref, z_s, sp_s, acc_s, carry_s):
    tq = q_ref.shape[1]
    tk = suffix_ref.shape[0]
    ratio = tq // tk
    suffix = suffix_ref[...]
    lane = lax.broadcasted_iota(jnp.int32, (tq, 2 * HEAD_DIM), 1)
    qp = q_ref[0]
    zero = jnp.zeros_like(qp)
    q_heads = (jnp.where(lane < HEAD_DIM, qp, zero), jnp.where(lane >= HEAD_DIM, qp, zero))

    def key_block(n):
        j = jnp.maximum(qi * ratio + (ratio - 1) - n, 0)
        return j, pl.ds(pl.multiple_of(j * tk, tk), tk)

    def first_row(n):
        return (ratio - 1 - n) * tk if is_own(n) else 0

    def is_own(n):
        return isinstance(n, int) and n < ratio

    def scores(n, slot):
        j, rows = key_block(n)
        r0 = first_row(n)
        kb = k_ref[0, rows, :]
        for hd in range(2):
            z = _dot_nt(q_heads[hd][r0:, :], kb)
            zb = z.astype(BF16)
            sp = jnp.maximum(zb, 0.0) + jnp.log(1.0 + jnp.exp2(jnp.abs(zb) * (-LOG2E)))
            if is_own(n):
                ri = lax.broadcasted_iota(jnp.int32, z.shape, 0)
                ci = lax.broadcasted_iota(jnp.int32, z.shape, 1)
                causal = ci < ri
                sp = jnp.where(causal, sp, jnp.zeros_like(sp))
                z = jnp.where(causal, z, MASKED_SCORE)
            z_s[hd, slot, r0:, :] = z
            sp_s[hd, slot, r0:, :] = sp

    def accumulate(n, slot):
        _, rows = key_block(n)
        r0 = first_row(n)
        vb = v_ref[0, rows, :]
        for hd in range(2):
            cs = _dot(sp_s[hd, slot, r0:, :], suffix)
            carry = carry_s[hd, r0:, :]
            a = jnp.exp(z_s[hd, slot, r0:, :] - cs - jnp.concatenate([carry] * (tk // 128), axis=1))
            acc_s[hd, r0:, :] += _dot(a.astype(BF16), vb)
            carry_s[hd, r0:, :] = carry + jnp.broadcast_to(cs[:, 0:1], carry.shape)

    acc_s[...] = jnp.zeros_like(acc_s)
    carry_s[...] = jnp.zeros_like(carry_s)
    scores(0, 0)
    for n in range(ratio):
        scores(n + 1, (n + 1) % 2)
        accumulate(n, n % 2)

    def run_blocks(n0, count):
        for u in range(count):
            scores(n0 + u + 1, (u + 1) % 2)
            accumulate(n0 + u, u % 2)

    n_groups = qi * (ratio // SB_UNROLL)
    odd = n_groups % 2

    @pl.when(odd == 1)
    def _():
        run_blocks(ratio, SB_UNROLL)

    def group(p, _):
        run_blocks(ratio + SB_UNROLL * odd + 2 * SB_UNROLL * p, 2 * SB_UNROLL)
        return 0

    lax.fori_loop(0, n_groups // 2, group, 0)
    o_ref[0] = jnp.where(lane < HEAD_DIM, acc_s[0], acc_s[1]).astype(BF16)


N_RWKV_IN = 13
N_RWKV_SCRATCH = 11


def _mix_kernel(*refs):
    rw_in, sb_in = refs[:N_RWKV_IN], refs[N_RWKV_IN:N_RWKV_IN + 4]
    y_ref, o_ref = refs[N_RWKV_IN + 4:N_RWKV_IN + 6]
    scratch = refs[N_RWKV_IN + 6:]
    n_q = sb_in[1].shape[1] // sb_in[0].shape[1]
    _rwkv_kernel(*rw_in, y_ref, *scratch[:N_RWKV_SCRATCH])
    _sb_kernel(pl.program_id(1) % n_q, *sb_in, o_ref, *scratch[N_RWKV_SCRATCH:])


def _token_mixers(f_rwkv, mu, w0, a0, k_k, k_a, r_k, lnx_g, lnx_b, wl, wg, bd_ones, q, k, v,
                  ts=256, tq=1024, tk=256):
    bsz, s, fw = f_rwkv.shape
    w = MIX_WIDTH
    pw = 2 * HEAD_DIM
    n_q = s // tq
    assert s // ts == (w // pw) * n_q
    assert (tq // tk) % SB_UNROLL == 0 and SB_UNROLL % 2 == 0
    ri = jnp.arange(ts)[:, None]
    ci = jnp.arange(ts)[None, :]
    same = (ri // CHUNK) == (ci // CHUNK)
    cum = (same & (ci <= ri)).astype(BF16)
    suffix = (jnp.arange(tk)[:, None] >= jnp.arange(tk)[None, :]).astype(BF16)
    vec = lambda n: pl.BlockSpec((1, n), lambda b, i: (0, 0))
    full = lambda a: pl.BlockSpec(a.shape, lambda b, i: (0, 0))
    return pl.pallas_call(
        _mix_kernel,
        grid=(bsz, s // ts),
        in_specs=[
            pl.BlockSpec((1, ts, fw), lambda b, i: (b, i, 0)),
            vec(fw), vec(w), vec(w), vec(w), vec(w), vec(w), vec(w), vec(w),
            full(wl), full(wg), full(bd_ones), full(cum),
            pl.BlockSpec((1, tq, pw), lambda b, i: (b, i % n_q, i // n_q)),
            pl.BlockSpec((1, s, pw), lambda b, i: (b, 0, i // n_q)),
            pl.BlockSpec((1, s, pw), lambda b, i: (b, 0, i // n_q)),
            pl.BlockSpec((tk, tk), lambda b, i: (0, 0)),
        ],
        out_specs=[
            pl.BlockSpec((1, ts, w), lambda b, i: (b, i, 0)),
            pl.BlockSpec((1, tq, pw), lambda b, i: (b, i % n_q, i // n_q)),
        ],
        out_shape=[jax.ShapeDtypeStruct((bsz, s, w), BF16), jax.ShapeDtypeStruct((bsz, s, w), BF16)],
        scratch_shapes=[
            pltpu.VMEM((1, fw), F32),
            pltpu.VMEM((N_HEADS, HEAD_DIM, HEAD_DIM), F32),
            pltpu.VMEM((ts, w), BF16), pltpu.VMEM((ts, w), BF16), pltpu.VMEM((ts, w), BF16),
            pltpu.VMEM((ts, w), BF16), pltpu.VMEM((ts, w), BF16), pltpu.VMEM((ts, w), BF16),
            pltpu.VMEM((ts, w), BF16),
            pltpu.VMEM((ts, w), F32),
            pltpu.VMEM((ts, w), F32),
            pltpu.VMEM((2, 2, tq, tk), F32),
            pltpu.VMEM((2, 2, tq, tk), BF16),
            pltpu.VMEM((2, tq, pw), F32),
            pltpu.VMEM((2, tq, pw), F32),
        ],
        compiler_params=_params(("arbitrary", "arbitrary")),
        name="token_mixers",
    )(f_rwkv, mu, w0, a0, k_k, k_a, r_k, lnx_g, lnx_b, wl, wg, bd_ones, cum, q, k, v, suffix)


def _merge_kernel(x_ref, yr_ref, ys_ref, gate_ref, bg_ref, mod_ref, wor_ref, wos_ref, wout_ref, o_ref):
    d = x_ref.shape[2]
    gr = _sigmoid(gate_ref[0, :, 0:d] + bg_ref[:, 0:d])
    gs = _sigmoid(gate_ref[0, :, d:2 * d] + bg_ref[:, d:2 * d])
    merged = gr * _dot(yr_ref[0], wor_ref[...]) + gs * _dot(ys_ref[0], wos_ref[...])
    o_ref[0] = x_ref[0] + mod_ref[0, 2:3, :] * _dot(merged.astype(BF16), wout_ref[...])


def _merge(x, y_rwkv, y_sb, gates, b_gate, mod, w_or, w_os, w_out, tm=512):
    bsz, s, d = x.shape
    full = lambda a: pl.BlockSpec(a.shape, lambda b, i: (0, 0))
    return pl.pallas_call(
        _merge_kernel,
        grid=(bsz, s // tm),
        in_specs=[
            pl.BlockSpec((1, tm, d), lambda b, i: (b, i, 0)),
            pl.BlockSpec((1, tm, MIX_WIDTH), lambda b, i: (b, i, 0)),
            pl.BlockSpec((1, tm, MIX_WIDTH), lambda b, i: (b, i, 0)),
            pl.BlockSpec((1, tm, 2 * d), lambda b, i: (b, i, 0)),
            pl.BlockSpec((1, 2 * d), lambda b, i: (0, 0)),
            pl.BlockSpec((1, 6, d), lambda b, i: (b, 0, 0)),
            full(w_or), full(w_os), full(w_out),
        ],
        out_specs=pl.BlockSpec((1, tm, d), lambda b, i: (b, i, 0)),
        out_shape=jax.ShapeDtypeStruct((bsz, s, d), F32),
        compiler_params=_params(("arbitrary", "arbitrary")),
        name="merge_out",
    )(x, y_rwkv, y_sb, gates, b_gate, mod, w_or, w_os, w_out)


def _ffn_kernel(x_ref, halo_ref, mod_ref, g_ref, wup_ref, cw_ref, cb_ref, wdn_ref, o_ref, act_s, *u_bufs, tf):
    tm = x_ref.shape[1]
    dff = wdn_ref.shape[0]
    x = x_ref[0]

    def norm_mod(t):
        ms = jnp.mean(t * t, axis=-1, keepdims=True)
        hh = t * lax.rsqrt(ms + RMS_EPS) * g_ref[...]
        return hh * (1.0 + mod_ref[0, 4:5, :]) + mod_ref[0, 3:4, :]

    h = norm_mod(x).astype(BF16)
    keep = (pl.program_id(1) > 0).astype(F32)
    hh = (norm_mod(halo_ref[0]) * keep).astype(BF16)
    hcat = jnp.concatenate([hh, h], axis=0)

    n_tiles = dff // tf

    def up(j):
        for half in range(2):
            col0 = half * dff + j * tf
            u_bufs[(2 * j + half) % len(u_bufs)][...] = _dot(hcat, wup_ref[:, col0:col0 + tf])

    def conv(j, half):
        cols = slice(half * dff + j * tf, half * dff + (j + 1) * tf)
        u_s = u_bufs[(2 * j + half) % len(u_bufs)]
        cw = cw_ref[:, cols]
        out = cb_ref[:, cols] + cw[2:3, :] * u_s[CONV_HALO:, :]
        out = out + cw[1:2, :] * u_s[CONV_HALO - 1:CONV_HALO - 1 + tm, :]
        out = out + cw[0:1, :] * u_s[CONV_HALO - 2:CONV_HALO - 2 + tm, :]
        return out

    def act(j):
        gt = conv(j, 1)
        act_s[:, j * tf:(j + 1) * tf] = (gt * _sigmoid(gt) * conv(j, 0)).astype(BF16)

    up(0)
    if n_tiles > 1:
        up(1)
    act(0)
    for j in range(n_tiles - 1):
        if j + 2 < n_tiles:
            up(j + 2)
        act(j + 1)
    o_ref[0] = x + mod_ref[0, 5:6, :] * _dot(act_s[...], wdn_ref[...])


def _ffn(x, mod, norm2_g, w_up, conv_w, conv_b, w_down, tm=512, tf=MXU_TILE):
    bsz, s, d = x.shape
    dff = w_down.shape[0]
    pad = -dff % tf
    halves = lambda a: jnp.concatenate(
        [jnp.pad(a[..., :dff], ((0, 0), (0, pad))), jnp.pad(a[..., dff:], ((0, 0), (0, pad)))], axis=-1)
    w_up = halves(w_up).astype(BF16)
    conv_w = halves(conv_w)
    conv_b = halves(conv_b)
    w_down = jnp.pad(w_down, ((0, pad), (0, 0))).astype(BF16)
    hb = tm // CONV_HALO
    full = lambda a: pl.BlockSpec(a.shape, lambda b, i: (0, 0), pipeline_mode=pl.Buffered(1))
    return pl.pallas_call(
        functools.partial(_ffn_kernel, tf=tf),
        grid=(bsz, s // tm),
        in_specs=[
            pl.BlockSpec((1, tm, d), lambda b, i: (b, i, 0)),
            pl.BlockSpec((1, CONV_HALO, d), lambda b, i: (b, jnp.maximum(i * hb - 1, 0), 0)),
            pl.BlockSpec((1, 6, d), lambda b, i: (b, 0, 0)),
            pl.BlockSpec((1, d), lambda b, i: (0, 0)),
            full(w_up), full(conv_w), full(conv_b), full(w_down),
        ],
        out_specs=pl.BlockSpec((1, tm, d), lambda b, i: (b, i, 0)),
        out_shape=jax.ShapeDtypeStruct((bsz, s, d), F32),
        scratch_shapes=[pltpu.VMEM((tm, dff + pad), BF16)]
        + [pltpu.VMEM((CONV_HALO + tm, tf), F32) for _ in range(4)],
        compiler_params=_params(("arbitrary", "arbitrary")),
        name="conv_ffn",
    )(x, x, mod, norm2_g, w_up, conv_w, conv_b, w_down)


def _layer(x, c, w_ada, b_ada, norm1_g, w_in, b_gate, rwkv_mu, rwkv_w0, rwkv_w2, rwkv_a0,
           rwkv_a2, rwkv_g2, rwkv_k_k, rwkv_k_a, rwkv_r_k, rwkv_lnx_g, rwkv_lnx_b, sb_q_g,
           sb_k_g, w_o_rwkv, w_o_sb, w_out, norm2_g, w_up, conv_w, conv_b, w_down):
    bsz, s, d = x.shape
    w = MIX_WIDTH
    rwkv_in = 3 * w + DECAY_LORA + ICLR_LORA + GATE_LORA
    sb_in = 3 * w
    pad = RWKV_PAD - rwkv_in

    w_all = jnp.concatenate([
        w_in[:, :rwkv_in], jnp.zeros((d, pad), F32),
        w_in[:, rwkv_in + sb_in:], w_in[:, rwkv_in:rwkv_in + sb_in]], axis=1).astype(BF16)
    mu = jnp.concatenate([rwkv_mu, jnp.zeros((pad,), F32)]).reshape(1, RWKV_PAD)
    wl = jnp.zeros((DECAY_LORA + ICLR_LORA, 2 * w), F32)
    wl = wl.at[:DECAY_LORA, :w].set(rwkv_w2).at[DECAY_LORA:, w:].set(rwkv_a2).astype(BF16)
    wg = jnp.zeros((256, w), F32).at[:GATE_LORA].set(rwkv_g2).astype(BF16)
    head_id = jnp.arange(w) // HEAD_DIM
    bd_ones = (head_id[:, None] == head_id[None, :]).astype(BF16)
    row = lambda a: a.reshape(1, -1)

    mod = _modulation(c, w_ada, b_ada).reshape(bsz, 6, d)
    f_rwkv, gates, q, k, v = _in_projection(
        x, mod, row(norm1_g), w_all, bd_ones,
        row(jnp.tile(sb_q_g, N_HEADS)), row(jnp.tile(sb_k_g, N_HEADS)))
    y_rwkv, y_sb = _token_mixers(
        f_rwkv, mu, row(rwkv_w0), row(rwkv_a0), row(rwkv_k_k), row(rwkv_k_a), row(rwkv_r_k),
        row(rwkv_lnx_g), row(rwkv_lnx_b), wl, wg, bd_ones, q, k, v)
    x = _merge(x, y_rwkv, y_sb, gates, row(b_gate), mod,
               w_o_rwkv.astype(BF16), w_o_sb.astype(BF16), w_out.astype(BF16))
    return _ffn(x, mod, row(norm2_g), w_up, conv_w, row(conv_b), w_down)


def kernel(x, c, w_ada, b_ada, norm1_g, w_in, b_gate, rwkv_mu, rwkv_w0, rwkv_w2, rwkv_a0, rwkv_a2, rwkv_g2, rwkv_k_k, rwkv_k_a, rwkv_r_k, rwkv_lnx_g, rwkv_lnx_b, sb_q_g, sb_k_g, w_o_rwkv, w_o_sb, w_out, norm2_g, w_up, conv_w, conv_b, w_down):
    params = (w_ada, b_ada, norm1_g, w_in, b_gate, rwkv_mu, rwkv_w0, rwkv_w2, rwkv_a0, rwkv_a2,
              rwkv_g2, rwkv_k_k, rwkv_k_a, rwkv_r_k, rwkv_lnx_g, rwkv_lnx_b, sb_q_g, sb_k_g,
              w_o_rwkv, w_o_sb, w_out, norm2_g, w_up, conv_w, conv_b, w_down)
    for layer in range(w_ada.shape[0]):
        x = _layer(x, c, *(p[layer] for p in params))
    return x
```
